```python
import math
import jax
import jax.numpy as jnp
from jax import lax
import numpy as np

D_MODEL = 1024
BATCH = 8
SEQ = 4096
DEPTH = 4

CONV_WIDTH_MIX = D_MODEL // 2
CONV_GROUPS = 8
CONV_K = 3
DIFF_HEADS = 4
DIFF_HEAD_DIM = 64
DIFF_WIDTH = DIFF_HEADS * 2 * DIFF_HEAD_DIM
IN_WIDTH = 3 * CONV_WIDTH_MIX + 3 * DIFF_WIDTH
SPLITS = [CONV_WIDTH_MIX, 2 * CONV_WIDTH_MIX, 3 * CONV_WIDTH_MIX,
          3 * CONV_WIDTH_MIX + DIFF_WIDTH, 3 * CONV_WIDTH_MIX + 2 * DIFF_WIDTH]
Q_BLOCK = 128
ROPE_THETA = 10000.0
SUBLN_EPS = 1e-5
RWKV_HEAD = 64
RWKV_HEADS = D_MODEL // RWKV_HEAD
LORA_DECAY = 64
LORA_AAA = 64
LORA_MV = 32
LORA_GATE = 160
GN_EPS = 64e-5
D_FF = 2816
FFN_CONV_K = 3
DN_ALPHA = (2 * DEPTH) ** 0.25
DN_BETA = (8 * DEPTH) ** -0.25
LN_EPS = 1e-5
N_EVEN = (DEPTH + 1) // 2
N_ODD = DEPTH // 2

kernel_name = 'hybrid_shortconv_diffattn_rwkv7_deepnorm'


def layer_norm(x, g, b):
    xf = x.astype(jnp.float32)
    mu = jnp.mean(xf, -1, keepdims=True)
    var = jnp.mean(jnp.square(xf - mu), -1, keepdims=True)
    return ((xf - mu) * lax.rsqrt(var + LN_EPS) * g + b).astype(x.dtype)


def causal_dwconv(u, w):
    K, C = w.shape
    return lax.conv_general_dilated(
        u, w[:, None, :].astype(u.dtype), window_strides=(1,), padding=[(K - 1, 0)],
        dimension_numbers=('NWC', 'WIO', 'NWC'), feature_group_count=C)


def rope_tables(T, dim):
    inv = 1.0 / (ROPE_THETA ** (jnp.arange(0, dim, 2, dtype=jnp.float32) / dim))
    ang = jnp.arange(T, dtype=jnp.float32)[:, None] * inv[None, :]
    return jnp.cos(ang), jnp.sin(ang)


def apply_rope(x, cos, sin):
    c = cos[None, :, None, :]
    s = sin[None, :, None, :]
    x1, x2 = jnp.split(x, 2, axis=-1)
    return jnp.concatenate([x1 * c - x2 * s, x2 * c + x1 * s], axis=-1).astype(x.dtype)


def diff_attention(q, k, v, lam):
    T = q.shape[3]
    scale = DIFF_HEAD_DIM ** -0.5
    outs = []
    for i in range(T // Q_BLOCK):
        s0 = i * Q_BLOCK
        e = s0 + Q_BLOCK
        s = jnp.einsum('bhmqd,bhmkd->bhmqk', q[:, :, :, s0:e], k[:, :, :, :e],
                       preferred_element_type=jnp.float32) * scale
        mask = jnp.arange(e)[None, :] <= jnp.arange(s0, e)[:, None]
        p = jax.nn.softmax(jnp.where(mask, s, -jnp.inf), axis=-1)
        a = p[:, :, 0] - lam * p[:, :, 1]
        outs.append(jnp.einsum('bhqk,bhkd->bhqd', a.astype(v.dtype), v[:, :, :e]))
    return jnp.concatenate(outs, axis=2)


def even_mixer(x, w_in, conv_w, lam_q1, lam_k1, lam_q2, lam_k2, subln_g, w_out,
               lam_init, cos, sin):
    B, T, _ = x.shape
    h = x @ w_in
    gb, gc, xin, q, k, v = jnp.split(h, SPLITS, axis=-1)
    y_a = gb * causal_dwconv(gc * xin, conv_w)
    q = apply_rope(q.reshape(B, T, 2 * DIFF_HEADS, DIFF_HEAD_DIM), cos, sin)
    k = apply_rope(k.reshape(B, T, 2 * DIFF_HEADS, DIFF_HEAD_DIM), cos, sin)
    qh = q.reshape(B, T, DIFF_HEADS, 2, DIFF_HEAD_DIM).transpose(0, 2, 3, 1, 4)
    kh = k.reshape(B, T, DIFF_HEADS, 2, DIFF_HEAD_DIM).transpose(0, 2, 3, 1, 4)
    vh = v.reshape(B, T, DIFF_HEADS, 2 * DIFF_HEAD_DIM).transpose(0, 2, 1, 3)
    lam = (jnp.exp(jnp.sum(lam_q1.astype(jnp.float32) * lam_k1.astype(jnp.float32)))
           - jnp.exp(jnp.sum(lam_q2.astype(jnp.float32) * lam_k2.astype(jnp.float32)))
           + lam_init)
    o = diff_attention(qh, kh, vh, lam).astype(jnp.float32)
    o = o * lax.rsqrt(jnp.mean(jnp.square(o), -1, keepdims=True) + SUBLN_EPS) * subln_g
    o = o * (1.0 - lam_init)
    y_b = o.transpose(0, 2, 1, 3).reshape(B, T, DIFF_WIDTH).astype(x.dtype)
    return jnp.concatenate([y_a, y_b], axis=-1) @ w_out


def wkv7_scan(r, w, k, v, a, b):
    B, T, H, N = r.shape

    def step(S, inp):
        r_t, w_t, k_t, v_t, a_t, b_t = inp
        sa = jnp.einsum('bhvk,bhk->bhv', S, a_t)
        S = S * w_t[:, :, None, :] + sa[..., None] * b_t[:, :, None, :] + v_t[..., None] * k_t[:, :, None, :]
        return S, jnp.einsum('bhvk,bhk->bhv', S, r_t)

    xs = tuple(jnp.moveaxis(t.astype(jnp.float32), 1, 0) for t in (r, w, k, v, a, b))
    _, o = lax.scan(step, jnp.zeros((B, H, N, N), jnp.float32), xs)
    return jnp.moveaxis(o, 0, 1)


def rwkv7_time_mix(x, mix, w_r, w_k, w_v, w_o, w0, w1, w2, a0, a1, a2, g1, g2,
                   k_k, k_a, r_k, gn_g, gn_b, v_first, vres):
    B, T, C = x.shape
    H, N = RWKV_HEADS, RWKV_HEAD
    xx = jnp.pad(x, ((0, 0), (1, 0), (0, 0)))[:, :-1] - x
    xr, xw, xk, xv, xa, xg = [x + xx * mix[i] for i in range(6)]
    r = xr @ w_r
    k = xk @ w_k
    v = xv @ w_v
    w_log = -jax.nn.softplus(-(w0 + jnp.tanh(xw @ w1) @ w2)) - 0.5
    decay = jnp.exp(-jnp.exp(w_log.astype(jnp.float32)))
    if vres is None:
        v_first = v
    else:
        v0, v1, v2 = vres
        v = v + (v_first - v) * jax.nn.sigmoid(v0 + (xv @ v1) @ v2)
    a = jax.nn.sigmoid(a0 + (xa @ a1) @ a2)
    g = jax.nn.sigmoid(xg @ g1) @ g2
    kk = (k * k_k).astype(jnp.float32).reshape(B, T, H, N)
    kk = kk / jnp.maximum(jnp.sqrt(jnp.sum(jnp.square(kk), -1, keepdims=True)), 1e-12)
    k = k * (1 + (a - 1) * k_a)
    rh = r.reshape(B, T, H, N)
    kh = k.reshape(B, T, H, N)
    vh = v.reshape(B, T, H, N)
    ah = a.astype(jnp.float32).reshape(B, T, H, N)
    o = wkv7_scan(rh, decay.reshape(B, T, H, N), kh, vh, -kk, kk * ah)
    mu = jnp.mean(o, -1, keepdims=True)
    var = jnp.mean(jnp.square(o - mu), -1, keepdims=True)
    o = ((o - mu) * lax.rsqrt(var + GN_EPS)).reshape(B, T, C) * gn_g + gn_b
    bonus = jnp.sum((rh * kh * r_k).astype(jnp.float32), -1, keepdims=True) * vh.astype(jnp.float32)
    o = (o + bonus.reshape(B, T, C)).astype(x.dtype)
    return (o * g) @ w_o, v_first


def conv_glu_ffn(x, w_up, conv_w, conv_b, w_down):
    h = causal_dwconv(x @ w_up, conv_w) + conv_b
    gate, up = jnp.split(h, 2, axis=-1)
    return (jax.nn.silu(gate) * up) @ w_down


def setup_inputs(seed: int = 0) -> dict:
    key = jax.random.key(seed)
    ks = iter(jax.random.split(key, 48))
    f32 = jnp.float32
    D = D_MODEL

    def nrm(shape, scale):
        return jax.random.normal(next(ks), shape, f32) * scale

    def unif(shape, lo, hi):
        return jax.random.uniform(next(ks), shape, f32, lo, hi)

    x = nrm((BATCH, SEQ, D), 1.0)
    col_scale = jnp.concatenate([
        jnp.ones((2 * CONV_WIDTH_MIX,), f32), jnp.full((CONV_WIDTH_MIX,), DN_BETA, f32),
        jnp.ones((2 * DIFF_WIDTH,), f32), jnp.full((DIFF_WIDTH,), DN_BETA, f32)])
    ev_w_in = nrm((N_EVEN, D, IN_WIDTH), D ** -0.5) * col_scale
    ev_conv_w = nrm((N_EVEN, CONV_K, CONV_WIDTH_MIX), CONV_K ** -0.5)
    ev_lam_q1 = nrm((N_EVEN, DIFF_HEAD_DIM), 0.1)
    ev_lam_k1 = nrm((N_EVEN, DIFF_HEAD_DIM), 0.1)
    ev_lam_q2 = nrm((N_EVEN, DIFF_HEAD_DIM), 0.1)
    ev_lam_k2 = nrm((N_EVEN, DIFF_HEAD_DIM), 0.1)
    ev_subln_g = 1.0 + nrm((N_EVEN, 2 * DIFF_HEAD_DIM), 0.02)
    ev_w_out = nrm((N_EVEN, D, D), D ** -0.5 * DN_BETA)

    rw_mix = unif((N_ODD, 6, D), 0.0, 1.0)
    rw_w_r = nrm((N_ODD, D, D), D ** -0.5)
    rw_w_k = nrm((N_ODD, D, D), D ** -0.5)
    rw_w_v = nrm((N_ODD, D, D), D ** -0.5 * DN_BETA)
    rw_w_o = nrm((N_ODD, D, D), D ** -0.5 * DN_BETA)
    rw_w0 = unif((N_ODD, D), -5.0, 0.5)
    rw_w1 = nrm((N_ODD, D, LORA_DECAY), D ** -0.5)
    rw_w2 = nrm((N_ODD, LORA_DECAY, D), 0.1 * LORA_DECAY ** -0.5)
    rw_a0 = nrm((N_ODD, D), 0.1)
    rw_a1 = nrm((N_ODD, D, LORA_AAA), D ** -0.5)
    rw_a2 = nrm((N_ODD, LORA_AAA, D), 0.5 * LORA_AAA ** -0.5)
    rw_g1 = nrm((N_ODD, D, LORA_GATE), D ** -0.5)
    rw_g2 = nrm((N_ODD, LORA_GATE, D), LORA_GATE ** -0.5)
    rw_k_k = 0.85 + nrm((N_ODD, D), 0.02)
    rw_k_a = 1.0 + nrm((N_ODD, D), 0.02)
    rw_r_k = -0.04 + nrm((N_ODD, RWKV_HEADS, RWKV_HEAD), 0.02)
    rw_gn_g = 1.0 + nrm((N_ODD, D), 0.02)
    rw_gn_b = nrm((N_ODD, D), 0.02)
    rw_v0 = nrm((N_ODD - 1, D), 0.1)
    rw_v1 = nrm((N_ODD - 1, D, LORA_MV), D ** -0.5)
    rw_v2 = nrm((N_ODD - 1, LORA_MV, D), 0.5 * LORA_MV ** -0.5)

    ffn_w_up = nrm((DEPTH, D, 2 * D_FF), D ** -0.5 * DN_BETA)
    ffn_conv_w = nrm((DEPTH, FFN_CONV_K, 2 * D_FF), FFN_CONV_K ** -0.5)
    ffn_conv_b = nrm((DEPTH, 2 * D_FF), 0.02)
    ffn_w_down = nrm((DEPTH, D_FF, D), D_FF ** -0.5 * DN_BETA)

    ln1_g = 1.0 + nrm((DEPTH, D), 0.02)
    ln1_b = nrm((DEPTH, D), 0.02)
    ln2_g = 1.0 + nrm((DEPTH, D), 0.02)
    ln2_b = nrm((DEPTH, D), 0.02)
    return {
        'x': x,
        'ev_w_in': ev_w_in, 'ev_conv_w': ev_conv_w,
        'ev_lam_q1': ev_lam_q1, 'ev_lam_k1': ev_lam_k1, 'ev_lam_q2': ev_lam_q2, 'ev_lam_k2': ev_lam_k2,
        'ev_subln_g': ev_subln_g, 'ev_w_out': ev_w_out,
        'rw_mix': rw_mix, 'rw_w_r': rw_w_r, 'rw_w_k': rw_w_k, 'rw_w_v': rw_w_v, 'rw_w_o': rw_w_o,
        'rw_w0': rw_w0, 'rw_w1': rw_w1, 'rw_w2': rw_w2,
        'rw_a0': rw_a0, 'rw_a1': rw_a1, 'rw_a2': rw_a2,
        'rw_g1': rw_g1, 'rw_g2': rw_g2,
        'rw_k_k': rw_k_k, 'rw_k_a': rw_k_a, 'rw_r_k': rw_r_k,
        'rw_gn_g': rw_gn_g, 'rw_gn_b': rw_gn_b,
        'rw_v0': rw_v0, 'rw_v1': rw_v1, 'rw_v2': rw_v2,
        'ffn_w_up': ffn_w_up, 'ffn_conv_w': ffn_conv_w, 'ffn_conv_b': ffn_conv_b, 'ffn_w_down': ffn_w_down,
        'ln1_g': ln1_g, 'ln1_b': ln1_b, 'ln2_g': ln2_g, 'ln2_b': ln2_b,
    }


def reference(x, ev_w_in, ev_conv_w, ev_lam_q1, ev_lam_k1, ev_lam_q2, ev_lam_k2,
              ev_subln_g, ev_w_out,
              rw_mix, rw_w_r, rw_w_k, rw_w_v, rw_w_o, rw_w0, rw_w1, rw_w2,
              rw_a0, rw_a1, rw_a2, rw_g1, rw_g2, rw_k_k, rw_k_a, rw_r_k,
              rw_gn_g, rw_gn_b, rw_v0, rw_v1, rw_v2,
              ffn_w_up, ffn_conv_w, ffn_conv_b, ffn_w_down,
              ln1_g, ln1_b, ln2_g, ln2_b):
    T = x.shape[1]
    cos, sin = rope_tables(T, DIFF_HEAD_DIM)
    v_first = None
    for l in range(DEPTH):
        if l % 2 == 0:
            i = l // 2
            lam_init = 0.8 - 0.6 * math.exp(-0.3 * l)
            y = even_mixer(x, ev_w_in[i], ev_conv_w[i], ev_lam_q1[i], ev_lam_k1[i],
                           ev_lam_q2[i], ev_lam_k2[i], ev_subln_g[i], ev_w_out[i],
                           lam_init, cos, sin)
        else:
            j = l // 2
            vres = None if j == 0 else (rw_v0[j - 1], rw_v1[j - 1], rw_v2[j - 1])
            y, v_first = rwkv7_time_mix(
                x, rw_mix[j], rw_w_r[j], rw_w_k[j], rw_w_v[j], rw_w_o[j],
                rw_w0[j], rw_w1[j], rw_w2[j], rw_a0[j], rw_a1[j], rw_a2[j],
                rw_g1[j], rw_g2[j], rw_k_k[j], rw_k_a[j], rw_r_k[j],
                rw_gn_g[j], rw_gn_b[j], v_first, vres)
        x = layer_norm(DN_ALPHA * x + y, ln1_g[l], ln1_b[l])
        f = conv_glu_ffn(x, ffn_w_up[l], ffn_conv_w[l], ffn_conv_b[l], ffn_w_down[l])
        x = layer_norm(DN_ALPHA * x + f, ln2_g[l], ln2_b[l])
    return x
```

```python
import functools
import math

import jax
import jax.numpy as jnp
from jax import lax
from jax.experimental import pallas as pl
from jax.experimental.pallas import tpu as pltpu

F32 = jnp.float32
BF16 = jnp.bfloat16

DEPTH = 4
DN_ALPHA = (2 * DEPTH) ** 0.25
LN_EPS = 1e-5
CONV_WIDTH_MIX = 512
DIFF_HEADS = 4
DIFF_HEAD_DIM = 64
ROPE_THETA = 10000.0
SUBLN_EPS = 1e-5
RWKV_HEAD = 64
GN_EPS = 64e-5

LANES = 128
SUBLANES = 8
VMEM_LIMIT_BYTES = 56 * 1024 * 1024

TM_PROJ = 512
TM_EVEN = 512
TQ_ATT = 256
TM_RWKV = 256
WKV_CHUNK = 64
TM_FFN = 512
FC_FFN = 256


def _cparams(semantics):
    return pltpu.CompilerParams(dimension_semantics=semantics, vmem_limit_bytes=VMEM_LIMIT_BYTES)


def _bdot(a, b):
    return jnp.dot(a.astype(BF16), b.astype(BF16), preferred_element_type=F32)


def _bdot_nt(a, b):
    return lax.dot_general(a.astype(BF16), b.astype(BF16), (((1,), (1,)), ((), ())),
                           preferred_element_type=F32)


def _layer_norm_rows(z, g, b):
    mu = jnp.mean(z, -1, keepdims=True)
    zc = z - mu
    var = jnp.mean(zc * zc, -1, keepdims=True)
    return zc * lax.rsqrt(var + LN_EPS) * g + b


def _halo_index(tm):
    per = tm // SUBLANES
    return lambda i: jnp.maximum(i * per - 1, 0)


def _proj_res_ln_kernel(*refs, n_pairs):
    a_refs = refs[:n_pairs]
    w_refs = refs[n_pairs:2 * n_pairs]
    x_ref, g_ref, b_ref, o_ref = refs[2 * n_pairs:]
    y = None
    for a_ref, w_ref in zip(a_refs, w_refs):
        d = jnp.dot(a_ref[...], w_ref[...], preferred_element_type=F32)
        y = d if y is None else y + d
    z = DN_ALPHA * x_ref[...] + y
    o_ref[...] = _layer_norm_rows(z, g_ref[...], b_ref[...])


def _proj_res_ln(a_list, w_list, x, g, b):
    M, D = x.shape
    tm = TM_PROJ
    n = len(a_list)
    in_specs = ([pl.BlockSpec((tm, a.shape[1]), lambda i: (i, 0)) for a in a_list]
                + [pl.BlockSpec(w.shape, lambda i: (0, 0)) for w in w_list]
                + [pl.BlockSpec((tm, D), lambda i: (i, 0)),
                   pl.BlockSpec((1, D), lambda i: (0, 0)),
                   pl.BlockSpec((1, D), lambda i: (0, 0))])
    return pl.pallas_call(
        functools.partial(_proj_res_ln_kernel, n_pairs=n),
        grid=(M // tm,),
        in_specs=in_specs,
        out_specs=pl.BlockSpec((tm, D), lambda i: (i, 0)),
        out_shape=jax.ShapeDtypeStruct((M, D), F32),
        compiler_params=_cparams(("parallel",)),
        name="proj_res_ln",
    )(*a_list, *w_list, x, g.reshape(1, D), b.reshape(1, D))


def _rope_block(z, c, s, first_half):
    partner = jnp.where(first_half, pltpu.roll(z, 96, 1), pltpu.roll(z, 32, 1))
    return z * c + partner * s


def _even_in_kernel(x_ref, xh_ref, w_ref, cw_ref, cos_ref, sin_ref,
                    ya_ref, q_ref, k_ref, v_ref, ubuf, *, tm):
    i = pl.program_id(1)
    W = CONV_WIDTH_MIX
    xb = x_ref[0].astype(BF16)
    gb = jnp.dot(xb, w_ref[:, 0:W], preferred_element_type=F32)
    gc = jnp.dot(xb, w_ref[:, W:2 * W], preferred_element_type=F32)
    xin = jnp.dot(xb, w_ref[:, 2 * W:3 * W], preferred_element_type=F32)
    u = gc * xin
    xhb = xh_ref[0].astype(BF16)
    gch = jnp.dot(xhb, w_ref[:, W:2 * W], preferred_element_type=F32)
    xinh = jnp.dot(xhb, w_ref[:, 2 * W:3 * W], preferred_element_type=F32)
    ubuf[0:SUBLANES, :] = gch * xinh * (i > 0).astype(F32)
    ubuf[SUBLANES:, :] = u
    cw = cw_ref[...]
    conv = (cw[2:3] * u + cw[1:2] * ubuf[pl.ds(SUBLANES - 1, tm), :]
            + cw[0:1] * ubuf[pl.ds(SUBLANES - 2, tm), :])
    ya_ref[0] = (gb * conv).astype(BF16)

    lane = lax.broadcasted_iota(jnp.int32, (1, LANES), 1)
    first_half = (lane % DIFF_HEAD_DIM) < (DIFF_HEAD_DIM // 2)
    c = cos_ref[...]
    s = sin_ref[...]
    scale = DIFF_HEAD_DIM ** -0.5
    for blk in range(W // LANES):
        lo = 3 * W + blk * LANES
        qb = jnp.dot(xb, w_ref[:, lo:lo + LANES], preferred_element_type=F32)
        q_ref[0, :, blk * LANES:(blk + 1) * LANES] = (_rope_block(qb, c, s, first_half) * scale).astype(BF16)
        lo = 4 * W + blk * LANES
        kb = jnp.dot(xb, w_ref[:, lo:lo + LANES], preferred_element_type=F32)
        k_ref[0, :, blk * LANES:(blk + 1) * LANES] = _rope_block(kb, c, s, first_half).astype(BF16)
    v_ref[0] = jnp.dot(xb, w_ref[:, 5 * W:6 * W], preferred_element_type=F32).astype(BF16)


def _even_in(x, w_in, conv_w, cos_t, sin_t):
    B, T, D = x.shape
    tm = TM_EVEN
    W = CONV_WIDTH_MIX
    halo = _halo_index(tm)
    out = jax.ShapeDtypeStruct((B, T, W), BF16)
    ospec = pl.BlockSpec((1, tm, W), lambda b, i: (b, i, 0))
    return pl.pallas_call(
        functools.partial(_even_in_kernel, tm=tm),
        grid=(B, T // tm),
        in_specs=[pl.BlockSpec((1, tm, D), lambda b, i: (b, i, 0)),
                  pl.BlockSpec((1, SUBLANES, D), lambda b, i: (b, halo(i), 0)),
                  pl.BlockSpec(w_in.shape, lambda b, i: (0, 0)),
                  pl.BlockSpec(conv_w.shape, lambda b, i: (0, 0)),
                  pl.BlockSpec((tm, LANES), lambda b, i: (i, 0)),
                  pl.BlockSpec((tm, LANES), lambda b, i: (i, 0))],
        out_specs=[ospec, ospec, ospec, ospec],
        out_shape=[out, out, out, out],
        scratch_shapes=[pltpu.VMEM((tm + SUBLANES, W), F32)],
        compiler_params=_cparams(("parallel", "parallel")),
        name="even_in_proj",
    )(x, x, w_in, conv_w, cos_t, sin_t)


def _diff_attn_kernel(q_ref, k_ref, v_ref, lq1_ref, lk1_ref, lq2_ref, lk2_ref, g_ref, o_ref,
                      *, tq, lam_init):
    qi = pl.program_id(2)
    lam = (jnp.exp(jnp.sum(lq1_ref[...] * lk1_ref[...], keepdims=True))
           - jnp.exp(jnp.sum(lq2_ref[...] * lk2_ref[...], keepdims=True)) + lam_init)
    q = q_ref[0]
    lane = lax.broadcasted_iota(jnp.int32, (1, LANES), 1)
    map0 = lane < DIFF_HEAD_DIM
    zero = jnp.zeros_like(q)
    qs = jnp.concatenate([jnp.where(map0, q, zero), jnp.where(map0, zero, q)], axis=0)

    def chunk(j, carry, diagonal):
        m, l, acc = carry
        start = pl.multiple_of(j * tq, tq)
        kj = k_ref[0, pl.ds(start, tq), :]
        vj = v_ref[0, pl.ds(start, tq), :]
        s = lax.dot_general(qs, kj, (((1,), (1,)), ((), ())), preferred_element_type=F32)
        if diagonal:
            row = lax.broadcasted_iota(jnp.int32, (2 * tq, tq), 0) % tq
            col = lax.broadcasted_iota(jnp.int32, (2 * tq, tq), 1)
            s = jnp.where(col <= row, s, -1e30)
        m_new = jnp.maximum(m, jnp.max(s, -1, keepdims=True))
        p = jnp.exp(s - m_new)
        corr = jnp.exp(m - m_new)
        l = l * corr + jnp.sum(p, -1, keepdims=True)
        acc = acc * corr + jnp.dot(p.astype(BF16), vj, preferred_element_type=F32)
        return m_new, l, acc

    init = (jnp.full((2 * tq, 1), -1e30, F32), jnp.zeros((2 * tq, 1), F32),
            jnp.zeros((2 * tq, LANES), F32))
    carry = lax.fori_loop(0, qi, lambda j, c: chunk(j, c, False), init)
    _, l, acc = chunk(qi, carry, True)
    o = acc[:tq] / l[:tq] - lam * (acc[tq:] / l[tq:])
    ms = jnp.mean(o * o, -1, keepdims=True)
    o = o * lax.rsqrt(ms + SUBLN_EPS) * g_ref[...]
    o_ref[0] = (o * (1.0 - lam_init)).astype(BF16)


def _diff_attn(q, k, v, lq1, lk1, lq2, lk2, subln_g, lam_init):
    B, T, W = q.shape
    tq = TQ_ATT
    vec = lambda a: a.reshape(1, -1)
    vspec = lambda n: pl.BlockSpec((1, n), lambda b, h, i: (0, 0))
    return pl.pallas_call(
        functools.partial(_diff_attn_kernel, tq=tq, lam_init=lam_init),
        grid=(B, DIFF_HEADS, T // tq),
        in_specs=[pl.BlockSpec((1, tq, LANES), lambda b, h, i: (b, i, h)),
                  pl.BlockSpec((1, T, LANES), lambda b, h, i: (b, 0, h)),
                  pl.BlockSpec((1, T, LANES), lambda b, h, i: (b, 0, h)),
                  vspec(DIFF_HEAD_DIM), vspec(DIFF_HEAD_DIM), vspec(DIFF_HEAD_DIM), vspec(DIFF_HEAD_DIM),
                  vspec(2 * DIFF_HEAD_DIM)],
        out_specs=pl.BlockSpec((1, tq, LANES), lambda b, h, i: (b, i, h)),
        out_shape=jax.ShapeDtypeStruct((B, T, W), BF16),
        compiler_params=_cparams(("parallel", "parallel", "parallel")),
        name="diff_attention",
    )(q, k, v, vec(lq1), vec(lk1), vec(lq2), vec(lk2), vec(subln_g))


def _sigmoid(z):
    return 1.0 / (1.0 + jnp.exp(-z))


def _rwkv_proj_kernel(*refs, tm, has_vres):
    if has_vres:
        (x_ref, xh_ref, mix_ref, wr_ref, wk_ref, wv_ref, w0_ref, w1_ref, w2_ref, a0_ref, a1_ref, a2_ref,
         g1_ref, g2_ref, v0_ref, v1_ref, v2_ref, vf_ref,
         r_ref, k_ref, v_ref, lw_ref, a_ref, g_ref, xbuf) = refs
    else:
        (x_ref, xh_ref, mix_ref, wr_ref, wk_ref, wv_ref, w0_ref, w1_ref, w2_ref, a0_ref, a1_ref, a2_ref,
         g1_ref, g2_ref,
         r_ref, k_ref, v_ref, lw_ref, a_ref, g_ref, xbuf) = refs
    i = pl.program_id(1)
    x = x_ref[0]
    xbuf[0:SUBLANES, :] = xh_ref[0] * (i > 0).astype(F32)
    xbuf[SUBLANES:, :] = x
    xx = xbuf[pl.ds(SUBLANES - 1, tm), :] - x
    mix = mix_ref[...]

    def mixed(n):
        return (x + xx * mix[n:n + 1]).astype(BF16)

    r_ref[0] = jnp.dot(mixed(0), wr_ref[...], preferred_element_type=F32)
    xw = mixed(1)
    wpre = w0_ref[...] + _bdot(jnp.tanh(jnp.dot(xw, w1_ref[...], preferred_element_type=F32)), w2_ref[...])
    w_log = jnp.minimum(wpre, 0.0) - jnp.log(1.0 + jnp.exp(-jnp.abs(wpre))) - 0.5
    lw_ref[0] = -jnp.exp(w_log)
    k_ref[0] = jnp.dot(mixed(2), wk_ref[...], preferred_element_type=F32)
    xv = mixed(3)
    v = jnp.dot(xv, wv_ref[...], preferred_element_type=F32)
    if has_vres:
        gate = _sigmoid(v0_ref[...] + _bdot(jnp.dot(xv, v1_ref[...], preferred_element_type=F32), v2_ref[...]))
        v = v + (vf_ref[0] - v) * gate
    v_ref[0] = v
    xa = mixed(4)
    a_ref[0] = _sigmoid(a0_ref[...] + _bdot(jnp.dot(xa, a1_ref[...], preferred_element_type=F32), a2_ref[...]))
    xg = mixed(5)
    g_ref[0] = _bdot(_sigmoid(jnp.dot(xg, g1_ref[...], preferred_element_type=F32)), g2_ref[...]).astype(BF16)


def _rwkv_proj(x, mix, w_r, w_k, w_v, w0, w1, w2, a0, a1, a2, g1, g2, vres, v_first):
    B, T, D = x.shape
    tm = TM_RWKV
    halo = _halo_index(tm)
    has_vres = vres is not None
    full = lambda a: pl.BlockSpec(a.shape, lambda b, i: (0,) * a.ndim)
    tile = pl.BlockSpec((1, tm, D), lambda b, i: (b, i, 0))
    row = lambda a: a.reshape(1, D)
    args = [x, x, mix, w_r, w_k, w_v, row(w0), w1, w2, row(a0), a1, a2, g1, g2]
    if has_vres:
        v0, v1, v2 = vres
        args += [row(v0), v1, v2, v_first]
    in_specs = ([tile, pl.BlockSpec((1, SUBLANES, D), lambda b, i: (b, halo(i), 0))]
                + [full(a) for a in args[2:14]])
    if has_vres:
        in_specs += [full(args[14]), full(args[15]), full(args[16]), tile]
    f32o = jax.ShapeDtypeStruct((B, T, D), F32)
    return pl.pallas_call(
        functools.partial(_rwkv_proj_kernel, tm=tm, has_vres=has_vres),
        grid=(B, T // tm),
        in_specs=in_specs,
        out_specs=[tile] * 6,
        out_shape=[f32o, f32o, f32o, f32o, f32o, jax.ShapeDtypeStruct((B, T, D), BF16)],
        scratch_shapes=[pltpu.VMEM((tm + SUBLANES, D), F32)],
        compiler_params=_cparams(("parallel", "parallel")),
        name="rwkv_proj",
    )(*args)


def _stack_heads(z, head0):
    zero = jnp.zeros_like(z)
    return jnp.concatenate([jnp.where(head0, z, zero), jnp.where(head0, zero, z)], axis=0)


def _head_sum(z, head0):
    zero = jnp.zeros_like(z)
    s0 = jnp.sum(jnp.where(head0, z, zero), -1, keepdims=True)
    s1 = jnp.sum(jnp.where(head0, zero, z), -1, keepdims=True)
    return jnp.where(head0, s0, s1)


def _unit_lower_inverse(L, row, col, n_time):
    zero = jnp.zeros_like(L)
    eye = (row == col).astype(F32)
    t = eye + jnp.where((row // 2) == (col // 2), L, zero)
    s = 2
    while s < n_time:
        lower_left = ((row // (2 * s)) == (col // (2 * s))) & ((row // s) % 2 == 1) & ((col // s) % 2 == 0)
        off = jnp.where(lower_left, L, zero)
        t = t + _bdot(_bdot(t, off), t)
        s *= 2
    return t


def _wkv_kernel(r_ref, k_ref, v_ref, lw_ref, a_ref, g_ref, kk_ref, ka_ref, rk_ref, gng_ref, gnb_ref,
                o_ref, st_ref, *, C, n_pairs):
    @pl.when(pl.program_id(1) == 0)
    def _():
        st_ref[...] = jnp.zeros_like(st_ref)

    P = 2 * C
    lane = lax.broadcasted_iota(jnp.int32, (1, LANES), 1)
    head0 = lane < RWKV_HEAD
    row = lax.broadcasted_iota(jnp.int32, (P, P), 0)
    col = lax.broadcasted_iota(jnp.int32, (P, P), 1)
    same_head = (row // C) == (col // C)
    strict = same_head & (col < row)
    incl = same_head & (col <= row)
    trow = lax.broadcasted_iota(jnp.int32, (C, LANES), 0)
    zero_pp = jnp.zeros((P, P), F32)

    for p in range(n_pairs):
        sl = slice(p * LANES, (p + 1) * LANES)
        r = r_ref[0, :, sl]
        k = k_ref[0, :, sl]
        v = v_ref[0, :, sl]
        lw = lw_ref[0, :, sl]
        a = a_ref[0, :, sl]
        kk = k * kk_ref[:, sl]
        norm = jnp.sqrt(_head_sum(kk * kk, head0))
        kk = kk / jnp.maximum(norm, 1e-12)
        k2 = k * (1.0 + (a - 1.0) * ka_ref[:, sl])
        cum = lw
        sh = 1
        while sh < C:
            cum = cum + jnp.where(trow >= sh, pltpu.roll(cum, sh, 0), 0.0)
            sh *= 2
        cum_end = cum[C - 1:C, :]
        e_pos = jnp.exp(cum)
        e_neg = jnp.exp(-cum)
        e_tail = jnp.exp(cum_end - cum)
        a_s = _stack_heads(-kk * jnp.exp(cum - lw), head0)
        b_t = kk * a
        b_s = _stack_heads(b_t * e_neg, head0)
        k_s = _stack_heads(k2 * e_neg, head0)
        r_s = _stack_heads(r * e_pos, head0)
        bh_s = _stack_heads(b_t * e_tail, head0)
        kh_s = _stack_heads(k2 * e_tail, head0)
        v_s = _stack_heads(v, head0)

        gram = _bdot_nt(jnp.concatenate([a_s, r_s], 0), jnp.concatenate([b_s, k_s], 0))
        l_ab = jnp.where(strict, gram[:P, :P], zero_pp)
        l_ak = jnp.where(strict, gram[:P, P:], zero_pp)
        m_rb = jnp.where(incl, gram[P:, :P], zero_pp)
        m_rk = jnp.where(incl, gram[P:, P:], zero_pp)
        t_inv = _unit_lower_inverse(l_ab, row, col, C)
        ty = _bdot(t_inv, jnp.concatenate([a_s, _bdot(l_ak, v_s)], 1))
        w_s = ty[:, :LANES]
        u0_s = ty[:, LANES:]

        st = st_ref[p]
        z = _bdot_nt(jnp.concatenate([w_s, r_s], 0), st)
        u_s = z[:P] + u0_s
        uv = jnp.concatenate([u_s, v_s], 0)
        o_s = z[P:] + _bdot(jnp.concatenate([m_rb, m_rk], 1), uv)
        o = o_s[:C] + o_s[C:]
        st_ref[p] = st * jnp.exp(cum_end) + _bdot(uv.T, jnp.concatenate([bh_s, kh_s], 0))

        mu = _head_sum(o, head0) * (1.0 / RWKV_HEAD)
        oc = o - mu
        var = _head_sum(oc * oc, head0) * (1.0 / RWKV_HEAD)
        on = oc * lax.rsqrt(var + GN_EPS) * gng_ref[:, sl] + gnb_ref[:, sl]
        bonus = _head_sum(r * k2 * rk_ref[:, sl], head0) * v
        o_ref[0, :, sl] = ((on + bonus) * g_ref[0, :, sl].astype(F32)).astype(BF16)


def _wkv(r, k, v, lw, a, g, k_k, k_a, r_k, gn_g, gn_b):
    B, T, D = r.shape
    C = WKV_CHUNK
    n_pairs = D // LANES
    tile = pl.BlockSpec((1, C, D), lambda b, c: (b, c, 0))
    vspec = pl.BlockSpec((1, D), lambda b, c: (0, 0))
    row = lambda z: z.reshape(1, D)
    return pl.pallas_call(
        functools.partial(_wkv_kernel, C=C, n_pairs=n_pairs),
        grid=(B, T // C),
        in_specs=[tile] * 6 + [vspec] * 5,
        out_specs=tile,
        out_shape=jax.ShapeDtypeStruct((B, T, D), BF16),
        scratch_shapes=[pltpu.VMEM((n_pairs, LANES, LANES), F32)],
        compiler_params=_cparams(("parallel", "arbitrary")),
        name="wkv7",
    )(r, k, v, lw, a, g, row(k_k), row(k_a), row(r_k), row(gn_g), row(gn_b))


def _ffn_kernel(x_ref, xh_ref, wg_ref, wu_ref, cwg_ref, cwu_ref, cbg_ref, cbu_ref, wd_ref, g_ref, b_ref,
                o_ref, xb_ref, xhb_ref, acc_ref, hg_ref, hu_ref, *, tm):
    i = pl.program_id(1)
    j = pl.program_id(2)

    @pl.when(j == 0)
    def _():
        xb_ref[...] = x_ref[0].astype(BF16)
        xhb_ref[...] = (xh_ref[0] * (i > 0).astype(F32)).astype(BF16)
        acc_ref[...] = jnp.zeros_like(acc_ref)

    def conv_branch(w_ref, cw_ref, cb_ref, h_ref):
        h = jnp.dot(xb_ref[...], w_ref[...], preferred_element_type=F32)
        h_ref[0:SUBLANES, :] = jnp.dot(xhb_ref[...], w_ref[...], preferred_element_type=F32)
        h_ref[SUBLANES:, :] = h
        cw = cw_ref[...]
        return (cw[2:3] * h + cw[1:2] * h_ref[pl.ds(SUBLANES - 1, tm), :]
                + cw[0:1] * h_ref[pl.ds(SUBLANES - 2, tm), :] + cb_ref[...])

    gate = conv_branch(wg_ref, cwg_ref, cbg_ref, hg_ref)
    up = conv_branch(wu_ref, cwu_ref, cbu_ref, hu_ref)
    act = gate * _sigmoid(gate) * up
    acc_ref[...] += jnp.dot(act.astype(BF16), wd_ref[...], preferred_element_type=F32)

    @pl.when(j == pl.num_programs(2) - 1)
    def _():
        z = DN_ALPHA * x_ref[0] + acc_ref[...]
        o_ref[0] = _layer_norm_rows(z, g_ref[...], b_ref[...])


def _ffn(x, w_up, conv_w, conv_b, w_down, ln_g, ln_b):
    B, T, D = x.shape
    d_ff = w_down.shape[0]
    tm, fc = TM_FFN, FC_FFN
    nj = d_ff // fc
    halo = _halo_index(tm)
    conv_b = conv_b.reshape(1, 2 * d_ff)
    return pl.pallas_call(
        functools.partial(_ffn_kernel, tm=tm),
        grid=(B, T // tm, nj),
        in_specs=[pl.BlockSpec((1, tm, D), lambda b, i, j: (b, i, 0)),
                  pl.BlockSpec((1, SUBLANES, D), lambda b, i, j: (b, halo(i), 0)),
                  pl.BlockSpec((D, fc), lambda b, i, j: (0, j)),
                  pl.BlockSpec((D, fc), lambda b, i, j: (0, nj + j)),
                  pl.BlockSpec((3, fc), lambda b, i, j: (0, j)),
                  pl.BlockSpec((3, fc), lambda b, i, j: (0, nj + j)),
                  pl.BlockSpec((1, fc), lambda b, i, j: (0, j)),
                  pl.BlockSpec((1, fc), lambda b, i, j: (0, nj + j)),
                  pl.BlockSpec((fc, D), lambda b, i, j: (j, 0)),
                  pl.BlockSpec((1, D), lambda b, i, j: (0, 0)),
                  pl.BlockSpec((1, D), lambda b, i, j: (0, 0))],
        out_specs=pl.BlockSpec((1, tm, D), lambda b, i, j: (b, i, 0)),
        out_shape=jax.ShapeDtypeStruct((B, T, D), F32),
        scratch_shapes=[pltpu.VMEM((tm, D), BF16), pltpu.VMEM((SUBLANES, D), BF16),
                        pltpu.VMEM((tm, D), F32),
                        pltpu.VMEM((tm + SUBLANES, fc), F32), pltpu.VMEM((tm + SUBLANES, fc), F32)],
        compiler_params=_cparams(("parallel", "parallel", "arbitrary")),
        name="conv_glu_ffn",
    )(x, x, w_up, w_up, conv_w, conv_w, conv_b, conv_b, w_down, ln_g.reshape(1, D), ln_b.reshape(1, D))


def _rope_tables(T):
    half = DIFF_HEAD_DIM // 2
    inv = 1.0 / (ROPE_THETA ** (jnp.arange(0, DIFF_HEAD_DIM, 2, dtype=F32) / DIFF_HEAD_DIM))
    ang = jnp.arange(T, dtype=F32)[:, None] * inv[None, :]
    cos, sin = jnp.cos(ang), jnp.sin(ang)
    reps = LANES // DIFF_HEAD_DIM
    cos_t = jnp.tile(jnp.concatenate([cos, cos], -1), (1, reps))
    sin_t = jnp.tile(jnp.concatenate([-sin, sin], -1), (1, reps))
    assert cos_t.shape == (T, LANES) and half * 2 == DIFF_HEAD_DIM
    return cos_t, sin_t


def _even_layer(x, w_in, conv_w, lq1, lk1, lq2, lk2, subln_g, w_out, lam_init, cos_t, sin_t, ln_g, ln_b):
    B, T, D = x.shape
    W = CONV_WIDTH_MIX
    y_a, q, k, v = _even_in(x, w_in.astype(BF16), conv_w, cos_t, sin_t)
    y_b = _diff_attn(q, k, v, lq1, lk1, lq2, lk2, subln_g, lam_init)
    w_out = w_out.astype(BF16)
    out = _proj_res_ln([y_a.reshape(B * T, W), y_b.reshape(B * T, W)], [w_out[:W], w_out[W:]],
                       x.reshape(B * T, D), ln_g, ln_b)
    return out.reshape(B, T, D)


def _odd_layer(x, mix, w_r, w_k, w_v, w_o, w0, w1, w2, a0, a1, a2, g1, g2, k_k, k_a, r_k, gn_g, gn_b,
               v_first, vres, ln_g, ln_b):
    B, T, D = x.shape
    bf = lambda z: z.astype(BF16)
    if vres is not None:
        vres = (vres[0], bf(vres[1]), bf(vres[2]))
    r, k, v, lw, a, g = _rwkv_proj(x, mix, bf(w_r), bf(w_k), bf(w_v), w0, bf(w1), bf(w2), a0, bf(a1), bf(a2),
                                   bf(g1), bf(g2), vres, v_first)
    if vres is None:
        v_first = v
    o = _wkv(r, k, v, lw, a, g, k_k, k_a, r_k.reshape(-1), gn_g, gn_b)
    out = _proj_res_ln([o.reshape(B * T, D)], [bf(w_o)], x.reshape(B * T, D), ln_g, ln_b)
    return out.reshape(B, T, D), v_first


def kernel(x, ev_w_in, ev_conv_w, ev_lam_q1, ev_lam_k1, ev_lam_q2, ev_lam_k2, ev_subln_g, ev_w_out, rw_mix, rw_w_r, rw_w_k, rw_w_v, rw_w_o, rw_w0, rw_w1, rw_w2, rw_a0, rw_a1, rw_a2, rw_g1, rw_g2, rw_k_k, rw_k_a, rw_r_k, rw_gn_g, rw_gn_b, rw_v0, rw_v1, rw_v2, ffn_w_up, ffn_conv_w, ffn_conv_b, ffn_w_down, ln1_g, ln1_b, ln2_g, ln2_b):
    T = x.shape[1]
    cos_t, sin_t = _rope_tables(T)
    v_first = None
    for l in range(DEPTH):
        if l % 2 == 0:
            i = l // 2
            lam_init = 0.8 - 0.6 * math.exp(-0.3 * l)
            x = _even_layer(x, ev_w_in[i], ev_conv_w[i], ev_lam_q1[i], ev_lam_k1[i], ev_lam_q2[i],
                            ev_lam_k2[i], ev_subln_g[i], ev_w_out[i], lam_init, cos_t, sin_t,
                            ln1_g[l], ln1_b[l])
        else:
            j = l // 2
            vres = None if j == 0 else (rw_v0[j - 1], rw_v1[j - 1], rw_v2[j - 1])
            x, v_first = _odd_layer(x, rw_mix[j], rw_w_r[j], rw_w_k[j], rw_w_v[j], rw_w_o[j], rw_w0[j],
                                    rw_w1[j], rw_w2[j], rw_a0[j], rw_a1[j], rw_a2[j], rw_g1[j], rw_g2[j],
                                    rw_k_k[j], rw_k_a[j], rw_r_k[j], rw_gn_g[j], rw_gn_b[j],
                                    v_first, vres, ln1_g[l], ln1_b[l])
        x = _ffn(x, ffn_w_up[l].astype(BF16), ffn_conv_w[l], ffn_conv_b[l], ffn_w_down[l].astype(BF16),
                 ln2_g[l], ln2_b[l])
    return x
```

```python
import functools
import math

import jax
import jax.numpy as jnp
from jax import lax
from jax.experimental import pallas as pl
from jax.experimental.pallas import tpu as pltpu

F32 = jnp.float32
BF16 = jnp.bfloat16

DEPTH = 4
DN_ALPHA = (2 * DEPTH) ** 0.25
LN_EPS = 1e-5
CONV_WIDTH_MIX = 512
DIFF_HEADS = 4
DIFF_HEAD_DIM = 64
ROPE_THETA = 10000.0
SUBLN_EPS = 1e-5
RWKV_HEAD = 64
GN_EPS = 64e-5

LANES = 128
SUBLANES = 8
VMEM_LIMIT_BYTES = 56 * 1024 * 1024

TM_PROJ = 512
TM_EVEN = 512
TQ_ATT = 256
TM_RWKV = 256
WKV_CHUNK = 64
TM_FFN = 512
FC_FFN = 256


def _cparams(semantics):
    return pltpu.CompilerParams(dimension_semantics=semantics, vmem_limit_bytes=VMEM_LIMIT_BYTES)


def _bdot(a, b):
    return jnp.dot(a.astype(BF16), b.astype(BF16), preferred_element_type=F32)


def _bdot_nt(a, b):
    return lax.dot_general(a.astype(BF16), b.astype(BF16), (((1,), (1,)), ((), ())),
                           preferred_element_type=F32)


def _layer_norm_rows(z, g, b):
    mu = jnp.mean(z, -1, keepdims=True)
    zc = z - mu
    var = jnp.mean(zc * zc, -1, keepdims=True)
    return zc * lax.rsqrt(var + LN_EPS) * g + b


def _halo_index(tm):
    per = tm // SUBLANES
    return lambda i: jnp.maximum(i * per - 1, 0)


def _proj_res_ln_kernel(*refs, n_pairs):
    a_refs = refs[:n_pairs]
    w_refs = refs[n_pairs:2 * n_pairs]
    x_ref, g_ref, b_ref, o_ref = refs[2 * n_pairs:]
    y = None
    for a_ref, w_ref in zip(a_refs, w_refs):
        d = jnp.dot(a_ref[...], w_ref[...], preferred_element_type=F32)
        y = d if y is None else y + d
    z = DN_ALPHA * x_ref[...] + y
    o_ref[...] = _layer_norm_rows(z, g_ref[...], b_ref[...])


def _proj_res_ln(a_list, w_list, x, g, b):
    M, D = x.shape
    tm = TM_PROJ
    n = len(a_list)
    in_specs = ([pl.BlockSpec((tm, a.shape[1]), lambda i: (i, 0)) for a in a_list]
                + [pl.BlockSpec(w.shape, lambda i: (0, 0)) for w in w_list]
                + [pl.BlockSpec((tm, D), lambda i: (i, 0)),
                   pl.BlockSpec((1, D), lambda i: (0, 0)),
                   pl.BlockSpec((1, D), lambda i: (0, 0))])
    return pl.pallas_call(
        functools.partial(_proj_res_ln_kernel, n_pairs=n),
        grid=(M // tm,),
        in_specs=in_specs,
        out_specs=pl.BlockSpec((tm, D), lambda i: (i, 0)),
        out_shape=jax.ShapeDtypeStruct((M, D), F32),
        compiler_params=_cparams(("parallel",)),
        name="proj_res_ln",
    )(*a_list, *w_list, x, g.reshape(1, D), b.reshape(1, D))


def _rope_block(z, c, s, first_half):
    half = DIFF_HEAD_DIM // 2
    partner = jnp.where(first_half, pltpu.roll(z, LANES - half, 1), pltpu.roll(z, half, 1))
    return z * c + partner * s


def _even_in_kernel(x_ref, xh_ref, w_ref, cw_ref, cos_ref, sin_ref,
                    ya_ref, q_ref, k_ref, v_ref, ubuf, *, tm):
    i = pl.program_id(1)
    W = CONV_WIDTH_MIX
    xb = x_ref[0].astype(BF16)
    gb = jnp.dot(xb, w_ref[:, 0:W], preferred_element_type=F32)
    gc = jnp.dot(xb, w_ref[:, W:2 * W], preferred_element_type=F32)
    xin = jnp.dot(xb, w_ref[:, 2 * W:3 * W], preferred_element_type=F32)
    u = gc * xin
    xhb = xh_ref[0].astype(BF16)
    gch = jnp.dot(xhb, w_ref[:, W:2 * W], preferred_element_type=F32)
    xinh = jnp.dot(xhb, w_ref[:, 2 * W:3 * W], preferred_element_type=F32)
    ubuf[0:SUBLANES, :] = gch * xinh * (i > 0).astype(F32)
    ubuf[SUBLANES:, :] = u
    cw = cw_ref[...]
    conv = (cw[2:3] * u + cw[1:2] * ubuf[pl.ds(SUBLANES - 1, tm), :]
            + cw[0:1] * ubuf[pl.ds(SUBLANES - 2, tm), :])
    ya_ref[0] = (gb * conv).astype(BF16)

    lane = lax.broadcasted_iota(jnp.int32, (1, LANES), 1)
    first_half = (lane % DIFF_HEAD_DIM) < (DIFF_HEAD_DIM // 2)
    c = cos_ref[...]
    s = sin_ref[...]
    scale = DIFF_HEAD_DIM ** -0.5
    for blk in range(W // LANES):
        lo = 3 * W + blk * LANES
        qb = jnp.dot(xb, w_ref[:, lo:lo + LANES], preferred_element_type=F32)
        q_ref[0, :, blk * LANES:(blk + 1) * LANES] = (_rope_block(qb, c, s, first_half) * scale).astype(BF16)
        lo = 4 * W + blk * LANES
        kb = jnp.dot(xb, w_ref[:, lo:lo + LANES], preferred_element_type=F32)
        k_ref[0, :, blk * LANES:(blk + 1) * LANES] = _rope_block(kb, c, s, first_half).astype(BF16)
    v_ref[0] = jnp.dot(xb, w_ref[:, 5 * W:6 * W], preferred_element_type=F32).astype(BF16)


def _even_in(x, w_in, conv_w, cos_t, sin_t):
    B, T, D = x.shape
    tm = TM_EVEN
    W = CONV_WIDTH_MIX
    halo = _halo_index(tm)
    out = jax.ShapeDtypeStruct((B, T, W), BF16)
    ospec = pl.BlockSpec((1, tm, W), lambda b, i: (b, i, 0))
    return pl.pallas_call(
        functools.partial(_even_in_kernel, tm=tm),
        grid=(B, T // tm),
        in_specs=[pl.BlockSpec((1, tm, D), lambda b, i: (b, i, 0)),
                  pl.BlockSpec((1, SUBLANES, D), lambda b, i: (b, halo(i), 0)),
                  pl.BlockSpec(w_in.shape, lambda b, i: (0, 0)),
                  pl.BlockSpec(conv_w.shape, lambda b, i: (0, 0)),
                  pl.BlockSpec((tm, LANES), lambda b, i: (i, 0)),
                  pl.BlockSpec((tm, LANES), lambda b, i: (i, 0))],
        out_specs=[ospec, ospec, ospec, ospec],
        out_shape=[out, out, out, out],
        scratch_shapes=[pltpu.VMEM((tm + SUBLANES, W), F32)],
        compiler_params=_cparams(("parallel", "parallel")),
        name="even_in_proj",
    )(x, x, w_in, conv_w, cos_t, sin_t)


def _diff_attn_kernel(q_ref, k_ref, v_ref, lq1_ref, lk1_ref, lq2_ref, lk2_ref, g_ref, o_ref,
                      *, tq, lam_init):
    qi = pl.program_id(2)
    lam = (jnp.exp(jnp.sum(lq1_ref[...] * lk1_ref[...], keepdims=True))
           - jnp.exp(jnp.sum(lq2_ref[...] * lk2_ref[...], keepdims=True)) + lam_init)
    q = q_ref[0]
    lane = lax.broadcasted_iota(jnp.int32, (1, LANES), 1)
    map0 = lane < DIFF_HEAD_DIM
    zero = jnp.zeros_like(q)
    qs = jnp.concatenate([jnp.where(map0, q, zero), jnp.where(map0, zero, q)], axis=0)

    def chunk(j, carry, diagonal):
        m, l, acc = carry
        start = pl.multiple_of(j * tq, tq)
        kj = k_ref[0, pl.ds(start, tq), :]
        vj = v_ref[0, pl.ds(start, tq), :]
        s = lax.dot_general(qs, kj, (((1,), (1,)), ((), ())), preferred_element_type=F32)
        if diagonal:
            row = lax.broadcasted_iota(jnp.int32, (2 * tq, tq), 0) % tq
            col = lax.broadcasted_iota(jnp.int32, (2 * tq, tq), 1)
            s = jnp.where(col <= row, s, -1e30)
        m_new = jnp.maximum(m, jnp.max(s, -1, keepdims=True))
        p = jnp.exp(s - m_new)
        corr = jnp.exp(m - m_new)
        l = l * corr + jnp.sum(p, -1, keepdims=True)
        acc = acc * corr + jnp.dot(p.astype(BF16), vj, preferred_element_type=F32)
        return m_new, l, acc

    init = (jnp.full((2 * tq, 1), -1e30, F32), jnp.zeros((2 * tq, 1), F32),
            jnp.zeros((2 * tq, LANES), F32))
    carry = lax.fori_loop(0, qi, lambda j, c: chunk(j, c, False), init)
    _, l, acc = chunk(qi, carry, True)
    o = acc[:tq] / l[:tq] - lam * (acc[tq:] / l[tq:])
    ms = jnp.mean(o * o, -1, keepdims=True)
    o = o * lax.rsqrt(ms + SUBLN_EPS) * g_ref[...]
    o_ref[0] = (o * (1.0 - lam_init)).astype(BF16)


def _diff_attn(q, k, v, lq1, lk1, lq2, lk2, subln_g, lam_init):
    B, T, W = q.shape
    tq = TQ_ATT
    vec = lambda a: a.reshape(1, -1)
    vspec = lambda n: pl.BlockSpec((1, n), lambda b, h, i: (0, 0))
    return pl.pallas_call(
        functools.partial(_diff_attn_kernel, tq=tq, lam_init=lam_init),
        grid=(B, DIFF_HEADS, T // tq),
        in_specs=[pl.BlockSpec((1, tq, LANES), lambda b, h, i: (b, i, h)),
                  pl.BlockSpec((1, T, LANES), lambda b, h, i: (b, 0, h)),
                  pl.BlockSpec((1, T, LANES), lambda b, h, i: (b, 0, h)),
                  vspec(DIFF_HEAD_DIM), vspec(DIFF_HEAD_DIM), vspec(DIFF_HEAD_DIM), vspec(DIFF_HEAD_DIM),
                  vspec(2 * DIFF_HEAD_DIM)],
        out_specs=pl.BlockSpec((1, tq, LANES), lambda b, h, i: (b, i, h)),
        out_shape=jax.ShapeDtypeStruct((B, T, W), BF16),
        compiler_params=_cparams(("parallel", "parallel", "parallel")),
        name="diff_attention",
    )(q, k, v, vec(lq1), vec(lk1), vec(lq2), vec(lk2), vec(subln_g))


def _sigmoid(z):
    return 1.0 / (1.0 + jnp.exp(-z))


def _rwkv_proj_kernel(*refs, tm, has_vres):
    if has_vres:
        (x_ref, xh_ref, mix_ref, wr_ref, wk_ref, wv_ref, w0_ref, w1_ref, w2_ref, a0_ref, a1_ref, a2_ref,
         g1_ref, g2_ref, v0_ref, v1_ref, v2_ref, vf_ref,
         r_ref, k_ref, v_ref, lw_ref, a_ref, g_ref, xbuf) = refs
    else:
        (x_ref, xh_ref, mix_ref, wr_ref, wk_ref, wv_ref, w0_ref, w1_ref, w2_ref, a0_ref, a1_ref, a2_ref,
         g1_ref, g2_ref,
         r_ref, k_ref, v_ref, lw_ref, a_ref, g_ref, xbuf) = refs
    i = pl.program_id(1)
    x = x_ref[0]
    xbuf[0:SUBLANES, :] = xh_ref[0] * (i > 0).astype(F32)
    xbuf[SUBLANES:, :] = x
    xx = xbuf[pl.ds(SUBLANES - 1, tm), :] - x
    mix = mix_ref[...]

    def mixed(n):
        return (x + xx * mix[n:n + 1]).astype(BF16)

    r_ref[0] = jnp.dot(mixed(0), wr_ref[...], preferred_element_type=F32)
    xw = mixed(1)
    wpre = w0_ref[...] + _bdot(jnp.tanh(jnp.dot(xw, w1_ref[...], preferred_element_type=F32)), w2_ref[...])
    w_log = jnp.minimum(wpre, 0.0) - jnp.log(1.0 + jnp.exp(-jnp.abs(wpre))) - 0.5
    lw_ref[0] = -jnp.exp(w_log)
    k_ref[0] = jnp.dot(mixed(2), wk_ref[...], preferred_element_type=F32)
    xv = mixed(3)
    v = jnp.dot(xv, wv_ref[...], preferred_element_type=F32)
    if has_vres:
        gate = _sigmoid(v0_ref[...] + _bdot(jnp.dot(xv, v1_ref[...], preferred_element_type=F32), v2_ref[...]))
        v = v + (vf_ref[0] - v) * gate
    v_ref[0] = v
    xa = mixed(4)
    a_ref[0] = _sigmoid(a0_ref[...] + _bdot(jnp.dot(xa, a1_ref[...], preferred_element_type=F32), a2_ref[...]))
    xg = mixed(5)
    g_ref[0] = _bdot(_sigmoid(jnp.dot(xg, g1_ref[...], preferred_element_type=F32)), g2_ref[...]).astype(BF16)


def _rwkv_proj(x, mix, w_r, w_k, w_v, w0, w1, w2, a0, a1, a2, g1, g2, vres, v_first):
    B, T, D = x.shape
    tm = TM_RWKV
    halo = _halo_index(tm)
    has_vres = vres is not None
    full = lambda a: pl.BlockSpec(a.shape, lambda b, i: (0,) * a.ndim)
    tile = pl.BlockSpec((1, tm, D), lambda b, i: (b, i, 0))
    row = lambda a: a.reshape(1, D)
    args = [x, x, mix, w_r, w_k, w_v, row(w0), w1, w2, row(a0), a1, a2, g1, g2]
    if has_vres:
        v0, v1, v2 = vres
        args += [row(v0), v1, v2, v_first]
    in_specs = ([tile, pl.BlockSpec((1, SUBLANES, D), lambda b, i: (b, halo(i), 0))]
                + [full(a) for a in args[2:14]])
    if has_vres:
        in_specs += [full(args[14]), full(args[15]), full(args[16]), tile]
    f32o = jax.ShapeDtypeStruct((B, T, D), F32)
    return pl.pallas_call(
        functools.partial(_rwkv_proj_kernel, tm=tm, has_vres=has_vres),
        grid=(B, T // tm),
        in_specs=in_specs,
        out_specs=[tile] * 6,
        out_shape=[f32o, f32o, f32o, f32o, f32o, jax.ShapeDtypeStruct((B, T, D), BF16)],
        scratch_shapes=[pltpu.VMEM((tm + SUBLANES, D), F32)],
        compiler_params=_cparams(("parallel", "parallel")),
        name="rwkv_proj",
    )(*args)


def _stack_heads(z, head0):
    zero = jnp.zeros_like(z)
    return jnp.concatenate([jnp.where(head0, z, zero), jnp.where(head0, zero, z)], axis=0)


def _head_sum(z, head0):
    zero = jnp.zeros_like(z)
    s0 = jnp.sum(jnp.where(head0, z, zero), -1, keepdims=True)
    s1 = jnp.sum(jnp.where(head0, zero, z), -1, keepdims=True)
    return jnp.where(head0, s0, s1)


def _unit_lower_inverse(Ls, row, col, n_time):
    zero = jnp.zeros_like(Ls[0])
    eye = (row == col).astype(F32)
    base = (row // 2) == (col // 2)
    ts = [eye + jnp.where(base, L, zero) for L in Ls]
    s = 2
    while s < n_time:
        lower_left = ((row // (2 * s)) == (col // (2 * s))) & ((row // s) % 2 == 1) & ((col // s) % 2 == 0)
        offs = [_bdot(t, jnp.where(lower_left, L, zero)) for t, L in zip(ts, Ls)]
        ts = [t + _bdot(o, t) for t, o in zip(ts, offs)]
        s *= 2
    return ts


def _wkv_kernel(r_ref, k_ref, v_ref, lw_ref, a_ref, g_ref, kk_ref, ka_ref, rk_ref, gng_ref, gnb_ref,
                o_ref, st_ref, *, C, n_pairs):
    @pl.when(pl.program_id(1) == 0)
    def _():
        st_ref[...] = jnp.zeros_like(st_ref)

    P = 2 * C
    lane = lax.broadcasted_iota(jnp.int32, (1, LANES), 1)
    head0 = lane < RWKV_HEAD
    row = lax.broadcasted_iota(jnp.int32, (P, P), 0)
    col = lax.broadcasted_iota(jnp.int32, (P, P), 1)
    same_head = (row // C) == (col // C)
    strict = same_head & (col < row)
    incl = same_head & (col <= row)
    trow = lax.broadcasted_iota(jnp.int32, (C, LANES), 0)
    zero_pp = jnp.zeros((P, P), F32)
    pairs = range(n_pairs)
    lanes_of = lambda p: slice(p * LANES, (p + 1) * LANES)

    def prepare(p):
        sl = lanes_of(p)
        r = r_ref[0, :, sl]
        k = k_ref[0, :, sl]
        lw = lw_ref[0, :, sl]
        a = a_ref[0, :, sl]
        kk = k * kk_ref[:, sl]
        norm = jnp.sqrt(_head_sum(kk * kk, head0))
        kk = kk / jnp.maximum(norm, 1e-12)
        k2 = k * (1.0 + (a - 1.0) * ka_ref[:, sl])
        cum = lw
        sh = 1
        while sh < C:
            cum = cum + jnp.where(trow >= sh, pltpu.roll(cum, sh, 0), 0.0)
            sh *= 2
        cum_end = cum[C - 1:C, :]
        e_pos = jnp.exp(cum)
        e_neg = jnp.exp(-cum)
        e_tail = jnp.exp(cum_end - cum)
        b_t = kk * a
        return dict(
            a_s=_stack_heads(-kk * jnp.exp(cum - lw), head0),
            b_s=_stack_heads(b_t * e_neg, head0),
            k_s=_stack_heads(k2 * e_neg, head0),
            r_s=_stack_heads(r * e_pos, head0),
            bh_s=_stack_heads(b_t * e_tail, head0),
            kh_s=_stack_heads(k2 * e_tail, head0),
            v_s=_stack_heads(v_ref[0, :, sl], head0),
            decay_end=jnp.exp(cum_end))

    d = [prepare(p) for p in pairs]
    grams = [_bdot_nt(jnp.concatenate([d[p]["a_s"], d[p]["r_s"]], 0),
                      jnp.concatenate([d[p]["b_s"], d[p]["k_s"]], 0)) for p in pairs]
    l_ab = [jnp.where(strict, g[:P, :P], zero_pp) for g in grams]
    lv = [_bdot(jnp.where(strict, grams[p][:P, P:], zero_pp), d[p]["v_s"]) for p in pairs]
    m_r = [jnp.concatenate([jnp.where(incl, g[P:, :P], zero_pp), jnp.where(incl, g[P:, P:], zero_pp)], 1)
           for g in grams]
    t_inv = _unit_lower_inverse(l_ab, row, col, C)
    ty = [_bdot(t_inv[p], jnp.concatenate([d[p]["a_s"], lv[p]], 1)) for p in pairs]
    sts = [st_ref[p] for p in pairs]
    zs = [_bdot_nt(jnp.concatenate([ty[p][:, :LANES], d[p]["r_s"]], 0), sts[p]) for p in pairs]
    uvs = [jnp.concatenate([zs[p][:P] + ty[p][:, LANES:], d[p]["v_s"]], 0) for p in pairs]
    o_s = [zs[p][P:] + _bdot(m_r[p], uvs[p]) for p in pairs]
    for p in pairs:
        st_ref[p] = sts[p] * d[p]["decay_end"] + _bdot(uvs[p].T, jnp.concatenate([d[p]["bh_s"], d[p]["kh_s"]], 0))

    for p in pairs:
        sl = lanes_of(p)
        o = o_s[p][:C] + o_s[p][C:]
        r = r_ref[0, :, sl]
        k2 = k_ref[0, :, sl] * (1.0 + (a_ref[0, :, sl] - 1.0) * ka_ref[:, sl])
        mu = _head_sum(o, head0) * (1.0 / RWKV_HEAD)
        oc = o - mu
        var = _head_sum(oc * oc, head0) * (1.0 / RWKV_HEAD)
        on = oc * lax.rsqrt(var + GN_EPS) * gng_ref[:, sl] + gnb_ref[:, sl]
        bonus = _head_sum(r * k2 * rk_ref[:, sl], head0) * v_ref[0, :, sl]
        o_ref[0, :, sl] = ((on + bonus) * g_ref[0, :, sl].astype(F32)).astype(BF16)


def _wkv(r, k, v, lw, a, g, k_k, k_a, r_k, gn_g, gn_b):
    B, T, D = r.shape
    C = WKV_CHUNK
    n_pairs = D // LANES
    tile = pl.BlockSpec((1, C, D), lambda b, c: (b, c, 0))
    vspec = pl.BlockSpec((1, D), lambda b, c: (0, 0))
    row = lambda z: z.reshape(1, D)
    return pl.pallas_call(
        functools.partial(_wkv_kernel, C=C, n_pairs=n_pairs),
        grid=(B, T // C),
        in_specs=[tile] * 6 + [vspec] * 5,
        out_specs=tile,
        out_shape=jax.ShapeDtypeStruct((B, T, D), BF16),
        scratch_shapes=[pltpu.VMEM((n_pairs, LANES, LANES), F32)],
        compiler_params=_cparams(("parallel", "arbitrary")),
        name="wkv7",
    )(r, k, v, lw, a, g, row(k_k), row(k_a), row(r_k), row(gn_g), row(gn_b))


def _ffn_kernel(x_ref, xh_ref, wg_ref, wu_ref, cwg_ref, cwu_ref, cbg_ref, cbu_ref, wd_ref, g_ref, b_ref,
                o_ref, xb_ref, xhb_ref, acc_ref, hg_ref, hu_ref, *, tm):
    i = pl.program_id(1)
    j = pl.program_id(2)

    @pl.when(j == 0)
    def _():
        xb_ref[...] = x_ref[0].astype(BF16)
        xhb_ref[...] = (xh_ref[0] * (i > 0).astype(F32)).astype(BF16)
        acc_ref[...] = jnp.zeros_like(acc_ref)

    def conv_branch(w_ref, cw_ref, cb_ref, h_ref):
        h = jnp.dot(xb_ref[...], w_ref[...], preferred_element_type=F32)
        h_ref[0:SUBLANES, :] = jnp.dot(xhb_ref[...], w_ref[...], preferred_element_type=F32)
        h_ref[SUBLANES:, :] = h
        cw = cw_ref[...]
        return (cw[2:3] * h + cw[1:2] * h_ref[pl.ds(SUBLANES - 1, tm), :]
                + cw[0:1] * h_ref[pl.ds(SUBLANES - 2, tm), :] + cb_ref[...])

    gate = conv_branch(wg_ref, cwg_ref, cbg_ref, hg_ref)
    up = conv_branch(wu_ref, cwu_ref, cbu_ref, hu_ref)
    act = gate * _sigmoid(gate) * up
    acc_ref[...] += jnp.dot(act.astype(BF16), wd_ref[...], preferred_element_type=F32)

    @pl.when(j == pl.num_programs(2) - 1)
    def _():
        z = DN_ALPHA * x_ref[0] + acc_ref[...]
        o_ref[0] = _layer_norm_rows(z, g_ref[...], b_ref[...])


def _ffn(x, w_up, conv_w, conv_b, w_down, ln_g, ln_b):
    B, T, D = x.shape
    d_ff = w_down.shape[0]
    tm, fc = TM_FFN, FC_FFN
    nj = d_ff // fc
    halo = _halo_index(tm)
    conv_b = conv_b.reshape(1, 2 * d_ff)
    return pl.pallas_call(
        functools.partial(_ffn_kernel, tm=tm),
        grid=(B, T // tm, nj),
        in_specs=[pl.BlockSpec((1, tm, D), lambda b, i, j: (b, i, 0)),
                  pl.BlockSpec((1, SUBLANES, D), lambda b, i, j: (b, halo(i), 0)),
                  pl.BlockSpec((D, fc), lambda b, i, j: (0, j)),
                  pl.BlockSpec((D, fc), lambda b, i, j: (0, nj + j)),
                  pl.BlockSpec((3, fc), lambda b, i, j: (0, j)),
                  pl.BlockSpec((3, fc), lambda b, i, j: (0, nj + j)),
                  pl.BlockSpec((1, fc), lambda b, i, j: (0, j)),
                  pl.BlockSpec((1, fc), lambda b, i, j: (0, nj + j)),
                  pl.BlockSpec((fc, D), lambda b, i, j: (j, 0)),
                  pl.BlockSpec((1, D), lambda b, i, j: (0, 0)),
                  pl.BlockSpec((1, D), lambda b, i, j: (0, 0))],
        out_specs=pl.BlockSpec((1, tm, D), lambda b, i, j: (b, i, 0)),
        out_shape=jax.ShapeDtypeStruct((B, T, D), F32),
        scratch_shapes=[pltpu.VMEM((tm, D), BF16), pltpu.VMEM((SUBLANES, D), BF16),
                        pltpu.VMEM((tm, D), F32),
                        pltpu.VMEM((tm + SUBLANES, fc), F32), pltpu.VMEM((tm + SUBLANES, fc), F32)],
        compiler_params=_cparams(("parallel", "parallel", "arbitrary")),
        name="conv_glu_ffn",
    )(x, x, w_up, w_up, conv_w, conv_w, conv_b, conv_b, w_down, ln_g.reshape(1, D), ln_b.reshape(1, D))


def _rope_tables(T):
    half = DIFF_HEAD_DIM // 2
    inv = 1.0 / (ROPE_THETA ** (jnp.arange(0, DIFF_HEAD_DIM, 2, dtype=F32) / DIFF_HEAD_DIM))
    ang = jnp.arange(T, dtype=F32)[:, None] * inv[None, :]
    cos, sin = jnp.cos(ang), jnp.sin(ang)
    reps = LANES // DIFF_HEAD_DIM
    cos_t = jnp.tile(jnp.concatenate([cos, cos], -1), (1, reps))
    sin_t = jnp.tile(jnp.concatenate([-sin, sin], -1), (1, reps))
    assert cos_t.shape == (T, LANES) and half * 2 == DIFF_HEAD_DIM
    return cos_t, sin_t


def _even_layer(x, w_in, conv_w, lq1, lk1, lq2, lk2, subln_g, w_out, lam_init, cos_t, sin_t, ln_g, ln_b):
    B, T, D = x.shape
    W = CONV_WIDTH_MIX
    y_a, q, k, v = _even_in(x, w_in.astype(BF16), conv_w, cos_t, sin_t)
    y_b = _diff_attn(q, k, v, lq1, lk1, lq2, lk2, subln_g, lam_init)
    w_out = w_out.astype(BF16)
    out = _proj_res_ln([y_a.reshape(B * T, W), y_b.reshape(B * T, W)], [w_out[:W], w_out[W:]],
                       x.reshape(B * T, D), ln_g, ln_b)
    return out.reshape(B, T, D)


def _odd_layer(x, mix, w_r, w_k, w_v, w_o, w0, w1, w2, a0, a1, a2, g1, g2, k_k, k_a, r_k, gn_g, gn_b,
               v_first, vres, ln_g, ln_b):
    B, T, D = x.shape
    bf = lambda z: z.astype(BF16)
    if vres is not None:
        vres = (vres[0], bf(vres[1]), bf(vres[2]))
    r, k, v, lw, a, g = _rwkv_proj(x, mix, bf(w_r), bf(w_k), bf(w_v), w0, bf(w1), bf(w2), a0, bf(a1), bf(a2),
                                   bf(g1), bf(g2), vres, v_first)
    if vres is None:
        v_first = v
    o = _wkv(r, k, v, lw, a, g, k_k, k_a, r_k.reshape(-1), gn_g, gn_b)
    out = _proj_res_ln([o.reshape(B * T, D)], [bf(w_o)], x.reshape(B * T, D), ln_g, ln_b)
    return out.reshape(B, T, D), v_first


def kernel(x, ev_w_in, ev_conv_w, ev_lam_q1, ev_lam_k1, ev_lam_q2, ev_lam_k2, ev_subln_g, ev_w_out, rw_mix, rw_w_r, rw_w_k, rw_w_v, rw_w_o, rw_w0, rw_w1, rw_w2, rw_a0, rw_a1, rw_a2, rw_g1, rw_g2, rw_k_k, rw_k_a, rw_r_k, rw_gn_g, rw_gn_b, rw_v0, rw_v1, rw_v2, ffn_w_up, ffn_conv_w, ffn_conv_b, ffn_w_down, ln1_g, ln1_b, ln2_g, ln2_b):
    T = x.shape[1]
    cos_t, sin_t = _rope_tables(T)
    v_first = None
    for l in range(DEPTH):
        if l % 2 == 0:
            i = l // 2
            lam_init = 0.8 - 0.6 * math.exp(-0.3 * l)
            x = _even_layer(x, ev_w_in[i], ev_conv_w[i], ev_lam_q1[i], ev_lam_k1[i], ev_lam_q2[i],
                            ev_lam_k2[i], ev_subln_g[i], ev_w_out[i], lam_init, cos_t, sin_t,
                            ln1_g[l], ln1_b[l])
        else:
            j = l // 2
            vres = None if j == 0 else (rw_v0[j - 1], rw_v1[j - 1], rw_v2[j - 1])
            x, v_first = _odd_layer(x, rw_mix[j], rw_w_r[j], rw_w_k[j], rw_w_v[j], rw_w_o[j], rw_w0[j],
                                    rw_w1[j], rw_w2[j], rw_a0[j], rw_a1[j], rw_a2[j], rw_g1[j], rw_g2[j],
                                    rw_k_k[j], rw_k_a[j], rw_r_k[j], rw_gn_g[j], rw_gn_b[j],
                                    v_first, vres, ln1_g[l], ln1_b[l])
        x = _ffn(x, ffn_w_up[l].astype(BF16), ffn_conv_w[l], ffn_conv_b[l], ffn_w_down[l].astype(BF16),
                 ln2_g[l], ln2_b[l])
    return x
```

```python
import functools
import math

import jax
import jax.numpy as jnp
from jax import lax
from jax.experimental import pallas as pl
from jax.experimental.pallas import tpu as pltpu

F32 = jnp.float32
BF16 = jnp.bfloat16

DEPTH = 4
DN_ALPHA = (2 * DEPTH) ** 0.25
LN_EPS = 1e-5
CONV_WIDTH_MIX = 512
DIFF_HEADS = 4
DIFF_HEAD_DIM = 64
ROPE_THETA = 10000.0
SUBLN_EPS = 1e-5
RWKV_HEAD = 64
GN_EPS = 64e-5

LANES = 128
SUBLANES = 8
HALO = 16
VMEM_LIMIT_BYTES = 56 * 1024 * 1024

TM_PROJ = 512
TM_EVEN = 512
TQ_ATT = 256
ATT_HEAD_GROUP = 2
TM_RWKV = 256
WKV_CHUNK = 64
WKV_BATCH = 2
TM_FFN = 512
FC_FFN = 256


def _cparams(semantics):
    return pltpu.CompilerParams(dimension_semantics=semantics, vmem_limit_bytes=VMEM_LIMIT_BYTES)


def _bdot(a, b):
    return jnp.dot(a.astype(BF16), b.astype(BF16), preferred_element_type=F32)


def _bdot_nt(a, b):
    return lax.dot_general(a.astype(BF16), b.astype(BF16), (((1,), (1,)), ((), ())),
                           preferred_element_type=F32)


def _layer_norm_rows(z, g, b):
    mu = jnp.mean(z, -1, keepdims=True)
    zc = z - mu
    var = jnp.mean(zc * zc, -1, keepdims=True)
    return zc * lax.rsqrt(var + LN_EPS) * g + b


def _halo_index(tm, rows):
    per = tm // rows
    return lambda i: jnp.maximum(i * per - 1, 0)


def _proj_res_ln_kernel(*refs, n_pairs):
    a_refs = refs[:n_pairs]
    w_refs = refs[n_pairs:2 * n_pairs]
    x_ref, g_ref, b_ref, o_ref = refs[2 * n_pairs:]
    y = None
    for a_ref, w_ref in zip(a_refs, w_refs):
        d = jnp.dot(a_ref[...], w_ref[...], preferred_element_type=F32)
        y = d if y is None else y + d
    z = DN_ALPHA * x_ref[...] + y
    o_ref[...] = _layer_norm_rows(z, g_ref[...], b_ref[...])


def _proj_res_ln(a_list, w_list, x, g, b):
    M, D = x.shape
    tm = TM_PROJ
    n = len(a_list)
    in_specs = ([pl.BlockSpec((tm, a.shape[1]), lambda i: (i, 0)) for a in a_list]
                + [pl.BlockSpec(w.shape, lambda i: (0, 0)) for w in w_list]
                + [pl.BlockSpec((tm, D), lambda i: (i, 0)),
                   pl.BlockSpec((1, D), lambda i: (0, 0)),
                   pl.BlockSpec((1, D), lambda i: (0, 0))])
    return pl.pallas_call(
        functools.partial(_proj_res_ln_kernel, n_pairs=n),
        grid=(M // tm,),
        in_specs=in_specs,
        out_specs=pl.BlockSpec((tm, D), lambda i: (i, 0)),
        out_shape=jax.ShapeDtypeStruct((M, D), F32),
        compiler_params=_cparams(("parallel",)),
        name="proj_res_ln",
    )(*a_list, *w_list, x, g.reshape(1, D), b.reshape(1, D))


def _rope_block(z, c, s, first_half):
    half = DIFF_HEAD_DIM // 2
    partner = jnp.where(first_half, pltpu.roll(z, LANES - half, 1), pltpu.roll(z, half, 1))
    return z * c + partner * s


def _even_in_kernel(x_ref, xh_ref, w_ref, cw_ref, cos_ref, sin_ref,
                    ya_ref, q_ref, k_ref, v_ref, ubuf, *, tm):
    i = pl.program_id(1)
    W = CONV_WIDTH_MIX
    xb = x_ref[0].astype(BF16)
    gb = jnp.dot(xb, w_ref[:, 0:W], preferred_element_type=F32)
    gc = jnp.dot(xb, w_ref[:, W:2 * W], preferred_element_type=F32)
    xin = jnp.dot(xb, w_ref[:, 2 * W:3 * W], preferred_element_type=F32)
    u = gc * xin
    xhb = xh_ref[0].astype(BF16)
    gch = jnp.dot(xhb, w_ref[:, W:2 * W], preferred_element_type=F32)
    xinh = jnp.dot(xhb, w_ref[:, 2 * W:3 * W], preferred_element_type=F32)
    ubuf[0:SUBLANES, :] = gch * xinh * (i > 0).astype(F32)
    ubuf[SUBLANES:, :] = u
    cw = cw_ref[...]
    conv = (cw[2:3] * u + cw[1:2] * ubuf[pl.ds(SUBLANES - 1, tm), :]
            + cw[0:1] * ubuf[pl.ds(SUBLANES - 2, tm), :])
    ya_ref[0] = (gb * conv).astype(BF16)

    lane = lax.broadcasted_iota(jnp.int32, (1, LANES), 1)
    first_half = (lane % DIFF_HEAD_DIM) < (DIFF_HEAD_DIM // 2)
    c = cos_ref[...]
    s = sin_ref[...]
    scale = DIFF_HEAD_DIM ** -0.5
    for blk in range(W // LANES):
        lo = 3 * W + blk * LANES
        qb = jnp.dot(xb, w_ref[:, lo:lo + LANES], preferred_element_type=F32)
        q_ref[0, :, blk * LANES:(blk + 1) * LANES] = (_rope_block(qb, c, s, first_half) * scale).astype(BF16)
        lo = 4 * W + blk * LANES
        kb = jnp.dot(xb, w_ref[:, lo:lo + LANES], preferred_element_type=F32)
        k_ref[0, :, blk * LANES:(blk + 1) * LANES] = _rope_block(kb, c, s, first_half).astype(BF16)
    v_ref[0] = jnp.dot(xb, w_ref[:, 5 * W:6 * W], preferred_element_type=F32).astype(BF16)


def _even_in(x, w_in, conv_w, cos_t, sin_t):
    B, T, D = x.shape
    tm = TM_EVEN
    W = CONV_WIDTH_MIX
    halo = _halo_index(tm, SUBLANES)
    out = jax.ShapeDtypeStruct((B, T, W), BF16)
    ospec = pl.BlockSpec((1, tm, W), lambda b, i: (b, i, 0))
    return pl.pallas_call(
        functools.partial(_even_in_kernel, tm=tm),
        grid=(B, T // tm),
        in_specs=[pl.BlockSpec((1, tm, D), lambda b, i: (b, i, 0)),
                  pl.BlockSpec((1, SUBLANES, D), lambda b, i: (b, halo(i), 0)),
                  pl.BlockSpec(w_in.shape, lambda b, i: (0, 0)),
                  pl.BlockSpec(conv_w.shape, lambda b, i: (0, 0)),
                  pl.BlockSpec((tm, LANES), lambda b, i: (i, 0)),
                  pl.BlockSpec((tm, LANES), lambda b, i: (i, 0))],
        out_specs=[ospec, ospec, ospec, ospec],
        out_shape=[out, out, out, out],
        scratch_shapes=[pltpu.VMEM((tm + SUBLANES, W), F32)],
        compiler_params=_cparams(("parallel", "parallel")),
        name="even_in_proj",
    )(x, x, w_in, conv_w, cos_t, sin_t)


def _diff_attn_kernel(q_ref, k_ref, v_ref, lq1_ref, lk1_ref, lq2_ref, lk2_ref, g_ref, o_ref,
                      qs_ref, m_ref, acc_ref, *, tq, lam_init):
    qi = pl.program_id(1)
    heads = range(DIFF_HEADS)
    lanes_of = lambda h: slice(h * LANES, (h + 1) * LANES)
    lam = (jnp.exp(jnp.sum(lq1_ref[...] * lk1_ref[...], keepdims=True))
           - jnp.exp(jnp.sum(lq2_ref[...] * lk2_ref[...], keepdims=True)) + lam_init)
    lane = lax.broadcasted_iota(jnp.int32, (1, LANES), 1)
    map0 = lane < DIFF_HEAD_DIM
    for h in heads:
        q = q_ref[0, :, lanes_of(h)]
        zero = jnp.zeros_like(q)
        qs_ref[h] = jnp.concatenate([jnp.where(map0, q, zero), jnp.where(map0, zero, q)], axis=0)
    m_ref[...] = jnp.full(m_ref.shape, -1e30, F32)
    acc_ref[...] = jnp.zeros(acc_ref.shape, F32)
    ones = jnp.ones((tq, LANES), BF16)

    def chunk(j, diagonal):
        for first in range(0, DIFF_HEADS, ATT_HEAD_GROUP):
            chunk_heads(j, diagonal, range(first, first + ATT_HEAD_GROUP))

    def chunk_heads(j, diagonal, group):
        start = pl.multiple_of(j * tq, tq)
        ss = {h: lax.dot_general(qs_ref[h], k_ref[0, pl.ds(start, tq), lanes_of(h)], (((1,), (1,)), ((), ())),
                                 preferred_element_type=F32) for h in group}
        if diagonal:
            row = lax.broadcasted_iota(jnp.int32, (2 * tq, tq), 0) % tq
            col = lax.broadcasted_iota(jnp.int32, (2 * tq, tq), 1)
            ss = {h: jnp.where(col <= row, s, -1e30) for h, s in ss.items()}
        m_old = {h: m_ref[h] for h in group}
        m_new = {h: jnp.maximum(m_old[h], jnp.max(ss[h], -1, keepdims=True)) for h in group}
        ps = {h: jnp.concatenate([jnp.exp(ss[h][:, t * LANES:(t + 1) * LANES] - m_new[h])
                                  for t in range(tq // LANES)], axis=1).astype(BF16) for h in group}
        corr = {h: jnp.exp(m_old[h] - m_new[h]) for h in group}
        pv = {h: jnp.dot(ps[h], jnp.concatenate([v_ref[0, pl.ds(start, tq), lanes_of(h)], ones], axis=1),
                         preferred_element_type=F32) for h in group}
        for h in group:
            m_ref[h] = m_new[h]
            acc_ref[h] = acc_ref[h] * jnp.concatenate([corr[h], corr[h]], axis=1) + pv[h]

    @pl.loop(0, qi)
    def _(j):
        chunk(j, False)

    chunk(qi, True)
    for h in heads:
        acc = acc_ref[h]
        num, den = acc[:, :LANES], acc[:, LANES:]
        o = num[:tq] / den[:tq] - lam * (num[tq:] / den[tq:])
        ms = jnp.mean(o * o, -1, keepdims=True)
        o = o * lax.rsqrt(ms + SUBLN_EPS) * g_ref[...]
        o_ref[0, :, lanes_of(h)] = (o * (1.0 - lam_init)).astype(BF16)


def _diff_attn(q, k, v, lq1, lk1, lq2, lk2, subln_g, lam_init):
    B, T, W = q.shape
    tq = TQ_ATT
    vec = lambda a: a.reshape(1, -1)
    vspec = lambda n: pl.BlockSpec((1, n), lambda b, i: (0, 0))
    return pl.pallas_call(
        functools.partial(_diff_attn_kernel, tq=tq, lam_init=lam_init),
        grid=(B, T // tq),
        in_specs=[pl.BlockSpec((1, tq, W), lambda b, i: (b, i, 0)),
                  pl.BlockSpec((1, T, W), lambda b, i: (b, 0, 0)),
                  pl.BlockSpec((1, T, W), lambda b, i: (b, 0, 0)),
                  vspec(DIFF_HEAD_DIM), vspec(DIFF_HEAD_DIM), vspec(DIFF_HEAD_DIM), vspec(DIFF_HEAD_DIM),
                  vspec(2 * DIFF_HEAD_DIM)],
        out_specs=pl.BlockSpec((1, tq, W), lambda b, i: (b, i, 0)),
        out_shape=jax.ShapeDtypeStruct((B, T, W), BF16),
        scratch_shapes=[pltpu.VMEM((DIFF_HEADS, 2 * tq, LANES), BF16),
                        pltpu.VMEM((DIFF_HEADS, 2 * tq, LANES), F32),
                        pltpu.VMEM((DIFF_HEADS, 2 * tq, 2 * LANES), F32)],
        compiler_params=_cparams(("parallel", "parallel")),
        name="diff_attention",
    )(q, k, v, vec(lq1), vec(lk1), vec(lq2), vec(lk2), vec(subln_g))


def _sigmoid(z):
    return 1.0 / (1.0 + jnp.exp(-z))


def _rwkv_proj_kernel(*refs, tm, has_vres):
    if has_vres:
        (x_ref, xh_ref, mix_ref, wr_ref, wk_ref, wv_ref, w0_ref, w1_ref, w2_ref, a0_ref, a1_ref, a2_ref,
         g1_ref, g2_ref, v0_ref, v1_ref, v2_ref, vf_ref,
         r_ref, k_ref, v_ref, lw_ref, a_ref, g_ref, xbuf) = refs
    else:
        (x_ref, xh_ref, mix_ref, wr_ref, wk_ref, wv_ref, w0_ref, w1_ref, w2_ref, a0_ref, a1_ref, a2_ref,
         g1_ref, g2_ref,
         r_ref, k_ref, v_ref, lw_ref, a_ref, g_ref, xbuf) = refs
    i = pl.program_id(1)
    x = x_ref[0]
    xbuf[0:SUBLANES, :] = xh_ref[0] * (i > 0).astype(F32)
    xbuf[SUBLANES:, :] = x
    xx = xbuf[pl.ds(SUBLANES - 1, tm), :] - x
    mix = mix_ref[...]

    def mixed(n):
        return (x + xx * mix[n:n + 1]).astype(BF16)

    r_ref[0] = jnp.dot(mixed(0), wr_ref[...], preferred_element_type=F32)
    xw = mixed(1)
    wpre = w0_ref[...] + _bdot(jnp.tanh(jnp.dot(xw, w1_ref[...], preferred_element_type=F32)), w2_ref[...])
    w_log = jnp.minimum(wpre, 0.0) - jnp.log(1.0 + jnp.exp(-jnp.abs(wpre))) - 0.5
    lw_ref[0] = -jnp.exp(w_log)
    k_ref[0] = jnp.dot(mixed(2), wk_ref[...], preferred_element_type=F32)
    xv = mixed(3)
    v = jnp.dot(xv, wv_ref[...], preferred_element_type=F32)
    if has_vres:
        gate = _sigmoid(v0_ref[...] + _bdot(jnp.dot(xv, v1_ref[...], preferred_element_type=F32), v2_ref[...]))
        v = v + (vf_ref[0] - v) * gate
    v_ref[0] = v
    xa = mixed(4)
    a_ref[0] = _sigmoid(a0_ref[...] + _bdot(jnp.dot(xa, a1_ref[...], preferred_element_type=F32), a2_ref[...]))
    xg = mixed(5)
    g_ref[0] = _bdot(_sigmoid(jnp.dot(xg, g1_ref[...], preferred_element_type=F32)), g2_ref[...]).astype(BF16)


def _rwkv_proj(x, mix, w_r, w_k, w_v, w0, w1, w2, a0, a1, a2, g1, g2, vres, v_first):
    B, T, D = x.shape
    tm = TM_RWKV
    halo = _halo_index(tm, SUBLANES)
    has_vres = vres is not None
    full = lambda a: pl.BlockSpec(a.shape, lambda b, i: (0,) * a.ndim)
    tile = pl.BlockSpec((1, tm, D), lambda b, i: (b, i, 0))
    row = lambda a: a.reshape(1, D)
    args = [x, x, mix, w_r, w_k, w_v, row(w0), w1, w2, row(a0), a1, a2, g1, g2]
    if has_vres:
        v0, v1, v2 = vres
        args += [row(v0), v1, v2, v_first]
    in_specs = ([tile, pl.BlockSpec((1, SUBLANES, D), lambda b, i: (b, halo(i), 0))]
                + [full(a) for a in args[2:14]])
    if has_vres:
        in_specs += [full(args[14]), full(args[15]), full(args[16]), tile]
    f32o = jax.ShapeDtypeStruct((B, T, D), F32)
    return pl.pallas_call(
        functools.partial(_rwkv_proj_kernel, tm=tm, has_vres=has_vres),
        grid=(B, T // tm),
        in_specs=in_specs,
        out_specs=[tile] * 6,
        out_shape=[f32o, f32o, f32o, f32o, f32o, jax.ShapeDtypeStruct((B, T, D), BF16)],
        scratch_shapes=[pltpu.VMEM((tm + SUBLANES, D), F32)],
        compiler_params=_cparams(("parallel", "parallel")),
        name="rwkv_proj",
    )(*args)


def _stack_heads(z, head0):
    zero = jnp.zeros_like(z)
    return jnp.concatenate([jnp.where(head0, z, zero), jnp.where(head0, zero, z)], axis=0)


def _head_sum(z, head0):
    zero = jnp.zeros_like(z)
    s0 = jnp.sum(jnp.where(head0, z, zero), -1, keepdims=True)
    s1 = jnp.sum(jnp.where(head0, zero, z), -1, keepdims=True)
    return jnp.where(head0, s0, s1)


def _unit_lower_inverse(Ls, row, col, n_time):
    zero = jnp.zeros_like(Ls[0])
    eye = (row == col).astype(F32)
    base = (row // 2) == (col // 2)
    ts = [eye + jnp.where(base, L, zero) for L in Ls]
    s = 2
    while s < n_time:
        lower_left = ((row // (2 * s)) == (col // (2 * s))) & ((row // s) % 2 == 1) & ((col // s) % 2 == 0)
        offs = [_bdot(t, jnp.where(lower_left, L, zero)) for t, L in zip(ts, Ls)]
        ts = [t + _bdot(o, t) for t, o in zip(ts, offs)]
        s *= 2
    return ts


def _wkv_kernel(r_ref, k_ref, v_ref, lw_ref, a_ref, g_ref, kk_ref, ka_ref, rk_ref, gng_ref, gnb_ref,
                o_ref, st_ref, *, C, n_pairs):
    @pl.when(pl.program_id(1) == 0)
    def _():
        st_ref[...] = jnp.zeros_like(st_ref)

    P = 2 * C
    lane = lax.broadcasted_iota(jnp.int32, (1, LANES), 1)
    head0 = lane < RWKV_HEAD
    row = lax.broadcasted_iota(jnp.int32, (P, P), 0)
    col = lax.broadcasted_iota(jnp.int32, (P, P), 1)
    same_head = (row // C) == (col // C)
    strict = same_head & (col < row)
    incl = same_head & (col <= row)
    trow = lax.broadcasted_iota(jnp.int32, (C, LANES), 0)
    zero_pp = jnp.zeros((P, P), F32)
    pairs = range(o_ref.shape[0] * n_pairs)
    lanes_of = lambda p: slice((p % n_pairs) * LANES, (p % n_pairs + 1) * LANES)

    def prepare(p):
        sl = lanes_of(p)
        r = r_ref[p // n_pairs, :, sl]
        k = k_ref[p // n_pairs, :, sl]
        lw = lw_ref[p // n_pairs, :, sl]
        a = a_ref[p // n_pairs, :, sl]
        kk = k * kk_ref[:, sl]
        norm = jnp.sqrt(_head_sum(kk * kk, head0))
        kk = kk / jnp.maximum(norm, 1e-12)
        k2 = k * (1.0 + (a - 1.0) * ka_ref[:, sl])
        cum = lw
        sh = 1
        while sh < C:
            cum = cum + jnp.where(trow >= sh, pltpu.roll(cum, sh, 0), 0.0)
            sh *= 2
        cum_end = cum[C - 1:C, :]
        e_pos = jnp.exp(cum)
        e_neg = jnp.exp(-cum)
        e_tail = jnp.exp(cum_end - cum)
        b_t = kk * a
        return dict(
            a_s=_stack_heads(-kk * jnp.exp(cum - lw), head0),
            b_s=_stack_heads(b_t * e_neg, head0),
            k_s=_stack_heads(k2 * e_neg, head0),
            r_s=_stack_heads(r * e_pos, head0),
            bh_s=_stack_heads(b_t * e_tail, head0),
            kh_s=_stack_heads(k2 * e_tail, head0),
            v_s=_stack_heads(v_ref[p // n_pairs, :, sl], head0),
            decay_end=jnp.exp(cum_end))

    d = [prepare(p) for p in pairs]
    grams = [_bdot_nt(jnp.concatenate([d[p]["a_s"], d[p]["r_s"]], 0),
                      jnp.concatenate([d[p]["b_s"], d[p]["k_s"]], 0)) for p in pairs]
    l_ab = [jnp.where(strict, g[:P, :P], zero_pp) for g in grams]
    lv = [_bdot(jnp.where(strict, grams[p][:P, P:], zero_pp), d[p]["v_s"]) for p in pairs]
    m_r = [jnp.concatenate([jnp.where(incl, g[P:, :P], zero_pp), jnp.where(incl, g[P:, P:], zero_pp)], 1)
           for g in grams]
    t_inv = _unit_lower_inverse(l_ab, row, col, C)
    ty = [_bdot(t_inv[p], jnp.concatenate([d[p]["a_s"], lv[p]], 1)) for p in pairs]
    sts = [st_ref[p] for p in pairs]
    zs = [_bdot_nt(jnp.concatenate([ty[p][:, :LANES], d[p]["r_s"]], 0), sts[p]) for p in pairs]
    uvs = [jnp.concatenate([zs[p][:P] + ty[p][:, LANES:], d[p]["v_s"]], 0) for p in pairs]
    o_s = [zs[p][P:] + _bdot(m_r[p], uvs[p]) for p in pairs]
    for p in pairs:
        st_ref[p] = sts[p] * d[p]["decay_end"] + _bdot(uvs[p].T, jnp.concatenate([d[p]["bh_s"], d[p]["kh_s"]], 0))

    for p in pairs:
        sl = lanes_of(p)
        o = o_s[p][:C] + o_s[p][C:]
        r = r_ref[p // n_pairs, :, sl]
        k2 = k_ref[p // n_pairs, :, sl] * (1.0 + (a_ref[p // n_pairs, :, sl] - 1.0) * ka_ref[:, sl])
        mu = _head_sum(o, head0) * (1.0 / RWKV_HEAD)
        oc = o - mu
        var = _head_sum(oc * oc, head0) * (1.0 / RWKV_HEAD)
        on = oc * lax.rsqrt(var + GN_EPS) * gng_ref[:, sl] + gnb_ref[:, sl]
        bonus = _head_sum(r * k2 * rk_ref[:, sl], head0) * v_ref[p // n_pairs, :, sl]
        o_ref[p // n_pairs, :, sl] = ((on + bonus) * g_ref[p // n_pairs, :, sl].astype(F32)).astype(BF16)


def _wkv(r, k, v, lw, a, g, k_k, k_a, r_k, gn_g, gn_b):
    B, T, D = r.shape
    C = WKV_CHUNK
    n_pairs = D // LANES
    nb = WKV_BATCH
    tile = pl.BlockSpec((nb, C, D), lambda b, c: (b, c, 0))
    vspec = pl.BlockSpec((1, D), lambda b, c: (0, 0))
    row = lambda z: z.reshape(1, D)
    return pl.pallas_call(
        functools.partial(_wkv_kernel, C=C, n_pairs=n_pairs),
        grid=(B // nb, T // C),
        in_specs=[tile] * 6 + [vspec] * 5,
        out_specs=tile,
        out_shape=jax.ShapeDtypeStruct((B, T, D), BF16),
        scratch_shapes=[pltpu.VMEM((nb * n_pairs, LANES, LANES), F32)],
        compiler_params=_cparams(("parallel", "arbitrary")),
        name="wkv7",
    )(r, k, v, lw, a, g, row(k_k), row(k_a), row(r_k), row(gn_g), row(gn_b))


def _ffn_kernel(x_ref, xh_ref, wup_ref, cw_ref, cb_ref, wd_ref, g_ref, b_ref,
                o_ref, xcat_ref, hg_ref, hu_ref, *, tm, fc, d_ff):
    i = pl.program_id(1)
    x = x_ref[0]
    xcat_ref[0:HALO, :] = (xh_ref[0] * (i > 0).astype(F32)).astype(BF16)
    xcat_ref[HALO:, :] = x.astype(BF16)

    def conv_branch(col, h_ref):
        h_ref[...] = jnp.dot(xcat_ref[...], wup_ref[:, col:col + fc], preferred_element_type=F32)
        cw = cw_ref[:, col:col + fc]
        return (cw[2:3] * h_ref[pl.ds(HALO, tm), :] + cw[1:2] * h_ref[pl.ds(HALO - 1, tm), :]
                + cw[0:1] * h_ref[pl.ds(HALO - 2, tm), :] + cb_ref[:, col:col + fc])

    y = None
    for c in range(d_ff // fc):
        slot = c % 2
        gate = conv_branch(c * fc, hg_ref.at[slot])
        up = conv_branch(d_ff + c * fc, hu_ref.at[slot])
        act = (gate * _sigmoid(gate) * up).astype(BF16)
        part = jnp.dot(act, wd_ref[c * fc:(c + 1) * fc, :], preferred_element_type=F32)
        y = part if y is None else y + part
    o_ref[0] = _layer_norm_rows(DN_ALPHA * x + y, g_ref[...], b_ref[...])


def _ffn(x, w_up, conv_w, conv_b, w_down, ln_g, ln_b):
    B, T, D = x.shape
    d_ff = w_down.shape[0]
    tm, fc = TM_FFN, FC_FFN
    assert d_ff % fc == 0
    halo = _halo_index(tm, HALO)
    const = lambda a: pl.BlockSpec(a.shape, lambda b, i: (0,) * a.ndim, pipeline_mode=pl.Buffered(1))
    conv_b = conv_b.reshape(1, 2 * d_ff)
    ln_g = ln_g.reshape(1, D)
    ln_b = ln_b.reshape(1, D)
    return pl.pallas_call(
        functools.partial(_ffn_kernel, tm=tm, fc=fc, d_ff=d_ff),
        grid=(B, T // tm),
        in_specs=[pl.BlockSpec((1, tm, D), lambda b, i: (b, i, 0)),
                  pl.BlockSpec((1, HALO, D), lambda b, i: (b, halo(i), 0)),
                  const(w_up), const(conv_w), const(conv_b), const(w_down), const(ln_g), const(ln_b)],
        out_specs=pl.BlockSpec((1, tm, D), lambda b, i: (b, i, 0)),
        out_shape=jax.ShapeDtypeStruct((B, T, D), F32),
        scratch_shapes=[pltpu.VMEM((tm + HALO, D), BF16),
                        pltpu.VMEM((2, tm + HALO, fc), F32), pltpu.VMEM((2, tm + HALO, fc), F32)],
        compiler_params=_cparams(("parallel", "parallel")),
        name="conv_glu_ffn",
    )(x, x, w_up, conv_w, conv_b, w_down, ln_g, ln_b)


def _rope_tables(T):
    half = DIFF_HEAD_DIM // 2
    inv = 1.0 / (ROPE_THETA ** (jnp.arange(0, DIFF_HEAD_DIM, 2, dtype=F32) / DIFF_HEAD_DIM))
    ang = jnp.arange(T, dtype=F32)[:, None] * inv[None, :]
    cos, sin = jnp.cos(ang), jnp.sin(ang)
    reps = LANES // DIFF_HEAD_DIM
    cos_t = jnp.tile(jnp.concatenate([cos, cos], -1), (1, reps))
    sin_t = jnp.tile(jnp.concatenate([-sin, sin], -1), (1, reps))
    assert cos_t.shape == (T, LANES) and half * 2 == DIFF_HEAD_DIM
    return cos_t, sin_t


def _even_layer(x, w_in, conv_w, lq1, lk1, lq2, lk2, subln_g, w_out, lam_init, cos_t, sin_t, ln_g, ln_b):
    B, T, D = x.shape
    W = CONV_WIDTH_MIX
    y_a, q, k, v = _even_in(x, w_in.astype(BF16), conv_w, cos_t, sin_t)
    y_b = _diff_attn(q, k, v, lq1, lk1, lq2, lk2, subln_g, lam_init)
    w_out = w_out.astype(BF16)
    out = _proj_res_ln([y_a.reshape(B * T, W), y_b.reshape(B * T, W)], [w_out[:W], w_out[W:]],
                       x.reshape(B * T, D), ln_g, ln_b)
    return out.reshape(B, T, D)


def _odd_layer(x, mix, w_r, w_k, w_v, w_o, w0, w1, w2, a0, a1, a2, g1, g2, k_k, k_a, r_k, gn_g, gn_b,
               v_first, vres, ln_g, ln_b):
    B, T, D = x.shape
    bf = lambda z: z.astype(BF16)
    if vres is not None:
        vres = (vres[0], bf(vres[1]), bf(vres[2]))
    r, k, v, lw, a, g = _rwkv_proj(x, mix, bf(w_r), bf(w_k), bf(w_v), w0, bf(w1), bf(w2), a0, bf(a1), bf(a2),
                                   bf(g1), bf(g2), vres, v_first)
    if vres is None:
        v_first = v
    o = _wkv(r, k, v, lw, a, g, k_k, k_a, r_k.reshape(-1), gn_g, gn_b)
    out = _proj_res_ln([o.reshape(B * T, D)], [bf(w_o)], x.reshape(B * T, D), ln_g, ln_b)
    return out.reshape(B, T, D), v_first


def kernel(x, ev_w_in, ev_conv_w, ev_lam_q1, ev_lam_k1, ev_lam_q2, ev_lam_k2, ev_subln_g, ev_w_out, rw_mix, rw_w_r, rw_w_k, rw_w_v, rw_w_o, rw_w0, rw_w1, rw_w2, rw_a0, rw_a1, rw_a2, rw_g1, rw_g2, rw_k_k, rw_k_a, rw_r_k, rw_gn_g, rw_gn_b, rw_v0, rw_v1, rw_v2, ffn_w_up, ffn_conv_w, ffn_conv_b, ffn_w_down, ln1_g, ln1_b, ln2_g, ln2_b):
    T = x.shape[1]
    cos_t, sin_t = _rope_tables(T)
    v_first = None
    for l in range(DEPTH):
        if l % 2 == 0:
            i = l // 2
            lam_init = 0.8 - 0.6 * math.exp(-0.3 * l)
            x = _even_layer(x, ev_w_in[i], ev_conv_w[i], ev_lam_q1[i], ev_lam_k1[i], ev_lam_q2[i],
                            ev_lam_k2[i], ev_subln_g[i], ev_w_out[i], lam_init, cos_t, sin_t,
                            ln1_g[l], ln1_b[l])
        else:
            j = l // 2
            vres = None if j == 0 else (rw_v0[j - 1], rw_v1[j - 1], rw_v2[j - 1])
            x, v_first = _odd_layer(x, rw_mix[j], rw_w_r[j], rw_w_k[j], rw_w_v[j], rw_w_o[j], rw_w0[j],
                                    rw_w1[j], rw_w2[j], rw_a0[j], rw_a1[j], rw_a2[j], rw_g1[j], rw_g2[j],
                                    rw_k_k[j], rw_k_a[j], rw_r_k[j], rw_gn_g[j], rw_gn_b[j],
                                    v_first, vres, ln1_g[l], ln1_b[l])
        x = _ffn(x, ffn_w_up[l].astype(BF16), ffn_conv_w[l], ffn_conv_b[l], ffn_w_down[l].astype(BF16),
                 ln2_g[l], ln2_b[l])
    return x
```

```python
import functools
import math

import jax
import jax.numpy as jnp
from jax import lax
from jax.experimental import pallas as pl
from jax.experimental.pallas import tpu as pltpu

F32 = jnp.float32
BF16 = jnp.bfloat16

DEPTH = 4
DN_ALPHA = (2 * DEPTH) ** 0.25
LN_EPS = 1e-5
CONV_WIDTH_MIX = 512
DIFF_HEADS = 4
DIFF_HEAD_DIM = 64
ROPE_THETA = 10000.0
SUBLN_EPS = 1e-5
LOG2_E = math.log2(math.e)
RWKV_HEAD = 64
GN_EPS = 64e-5

LANES = 128
SUBLANES = 8
HALO = 16
VMEM_LIMIT_BYTES = 56 * 1024 * 1024

TM_PROJ = 512
TM_EVEN = 512
TQ_ATT = 256
ATT_HEAD_GROUP = 4
TM_RWKV = 512
WKV_CHUNK = 64
WKV_BATCH = 2
TM_FFN = 512
FC_FFN = 256
FFN_DOWN_GROUP = 4


def _cparams(semantics):
    return pltpu.CompilerParams(dimension_semantics=semantics, vmem_limit_bytes=VMEM_LIMIT_BYTES)


def _bdot(a, b):
    return jnp.dot(a.astype(BF16), b.astype(BF16), preferred_element_type=F32)


def _bdot_nt(a, b):
    return lax.dot_general(a.astype(BF16), b.astype(BF16), (((1,), (1,)), ((), ())),
                           preferred_element_type=F32)


def _layer_norm_rows(z, g, b):
    mu = jnp.mean(z, -1, keepdims=True)
    zc = z - mu
    var = jnp.mean(zc * zc, -1, keepdims=True)
    return zc * lax.rsqrt(var + LN_EPS) * g + b


def _halo_index(tm, rows):
    per = tm // rows
    return lambda i: jnp.maximum(i * per - 1, 0)


def _proj_res_ln_kernel(*refs, n_pairs):
    a_refs = refs[:n_pairs]
    w_refs = refs[n_pairs:2 * n_pairs]
    x_ref, g_ref, b_ref, o_ref = refs[2 * n_pairs:]
    y = None
    for a_ref, w_ref in zip(a_refs, w_refs):
        d = jnp.dot(a_ref[...], w_ref[...], preferred_element_type=F32)
        y = d if y is None else y + d
    z = DN_ALPHA * x_ref[...] + y
    o_ref[...] = _layer_norm_rows(z, g_ref[...], b_ref[...])


def _proj_res_ln(a_list, w_list, x, g, b):
    M, D = x.shape
    tm = TM_PROJ
    n = len(a_list)
    in_specs = ([pl.BlockSpec((tm, a.shape[1]), lambda i: (i, 0)) for a in a_list]
                + [pl.BlockSpec(w.shape, lambda i: (0, 0)) for w in w_list]
                + [pl.BlockSpec((tm, D), lambda i: (i, 0)),
                   pl.BlockSpec((1, D), lambda i: (0, 0)),
                   pl.BlockSpec((1, D), lambda i: (0, 0))])
    return pl.pallas_call(
        functools.partial(_proj_res_ln_kernel, n_pairs=n),
        grid=(M // tm,),
        in_specs=in_specs,
        out_specs=pl.BlockSpec((tm, D), lambda i: (i, 0)),
        out_shape=jax.ShapeDtypeStruct((M, D), F32),
        compiler_params=_cparams(("parallel",)),
        name="proj_res_ln",
    )(*a_list, *w_list, x, g.reshape(1, D), b.reshape(1, D))


def _rope_block(z, c, s, first_half):
    half = DIFF_HEAD_DIM // 2
    partner = jnp.where(first_half, pltpu.roll(z, LANES - half, 1), pltpu.roll(z, half, 1))
    return z * c + partner * s


def _even_in_kernel(x_ref, xh_ref, w_ref, cw_ref, cos_ref, sin_ref,
                    ya_ref, q_ref, k_ref, v_ref, ubuf, *, tm):
    i = pl.program_id(1)
    W = CONV_WIDTH_MIX
    xb = x_ref[0].astype(BF16)
    gb = jnp.dot(xb, w_ref[:, 0:W], preferred_element_type=F32)
    gc = jnp.dot(xb, w_ref[:, W:2 * W], preferred_element_type=F32)
    xin = jnp.dot(xb, w_ref[:, 2 * W:3 * W], preferred_element_type=F32)
    u = gc * xin
    xhb = xh_ref[0].astype(BF16)
    gch = jnp.dot(xhb, w_ref[:, W:2 * W], preferred_element_type=F32)
    xinh = jnp.dot(xhb, w_ref[:, 2 * W:3 * W], preferred_element_type=F32)
    ubuf[0:SUBLANES, :] = gch * xinh * (i > 0).astype(F32)
    ubuf[SUBLANES:, :] = u
    cw = cw_ref[...]
    conv = (cw[2:3] * u + cw[1:2] * ubuf[pl.ds(SUBLANES - 1, tm), :]
            + cw[0:1] * ubuf[pl.ds(SUBLANES - 2, tm), :])
    ya_ref[0] = (gb * conv).astype(BF16)

    lane = lax.broadcasted_iota(jnp.int32, (1, LANES), 1)
    first_half = (lane % DIFF_HEAD_DIM) < (DIFF_HEAD_DIM // 2)
    c = cos_ref[...]
    s = sin_ref[...]
    scale = DIFF_HEAD_DIM ** -0.5 * LOG2_E
    for blk in range(W // LANES):
        lo = 3 * W + blk * LANES
        qb = jnp.dot(xb, w_ref[:, lo:lo + LANES], preferred_element_type=F32)
        q_ref[0, :, blk * LANES:(blk + 1) * LANES] = (_rope_block(qb, c, s, first_half) * scale).astype(BF16)
        lo = 4 * W + blk * LANES
        kb = jnp.dot(xb, w_ref[:, lo:lo + LANES], preferred_element_type=F32)
        k_ref[0, :, blk * LANES:(blk + 1) * LANES] = _rope_block(kb, c, s, first_half).astype(BF16)
    v_ref[0] = jnp.dot(xb, w_ref[:, 5 * W:6 * W], preferred_element_type=F32).astype(BF16)


def _even_in(x, w_in, conv_w, cos_t, sin_t):
    B, T, D = x.shape
    tm = TM_EVEN
    W = CONV_WIDTH_MIX
    halo = _halo_index(tm, SUBLANES)
    out = jax.ShapeDtypeStruct((B, T, W), BF16)
    ospec = pl.BlockSpec((1, tm, W), lambda b, i: (b, i, 0))
    return pl.pallas_call(
        functools.partial(_even_in_kernel, tm=tm),
        grid=(B, T // tm),
        in_specs=[pl.BlockSpec((1, tm, D), lambda b, i: (b, i, 0)),
                  pl.BlockSpec((1, SUBLANES, D), lambda b, i: (b, halo(i), 0)),
                  pl.BlockSpec(w_in.shape, lambda b, i: (0, 0)),
                  pl.BlockSpec(conv_w.shape, lambda b, i: (0, 0)),
                  pl.BlockSpec((tm, LANES), lambda b, i: (i, 0)),
                  pl.BlockSpec((tm, LANES), lambda b, i: (i, 0))],
        out_specs=[ospec, ospec, ospec, ospec],
        out_shape=[out, out, out, out],
        scratch_shapes=[pltpu.VMEM((tm + SUBLANES, W), F32)],
        compiler_params=_cparams(("parallel", "parallel")),
        name="even_in_proj",
    )(x, x, w_in, conv_w, cos_t, sin_t)


def _diff_attn_kernel(q_ref, k_ref, v_ref, lq1_ref, lk1_ref, lq2_ref, lk2_ref, g_ref, o_ref,
                      qs_ref, m_ref, l_ref, acc_ref, *, tq, lam_init):
    qi = pl.program_id(1)
    heads = range(DIFF_HEADS)
    lanes_of = lambda h: slice(h * LANES, (h + 1) * LANES)
    lam = (jnp.exp(jnp.sum(lq1_ref[...] * lk1_ref[...], keepdims=True))
           - jnp.exp(jnp.sum(lq2_ref[...] * lk2_ref[...], keepdims=True)) + lam_init)
    lane = lax.broadcasted_iota(jnp.int32, (1, LANES), 1)
    map0 = lane < DIFF_HEAD_DIM
    for h in heads:
        q = q_ref[0, :, lanes_of(h)]
        zero = jnp.zeros_like(q)
        qs_ref[h] = jnp.concatenate([jnp.where(map0, q, zero), jnp.where(map0, zero, q)], axis=0)
    m_ref[...] = jnp.full(m_ref.shape, -1e30, F32)
    l_ref[...] = jnp.zeros(l_ref.shape, F32)
    acc_ref[...] = jnp.zeros(acc_ref.shape, F32)

    def chunk(start, tk, diag_offset):
        for first in range(0, DIFF_HEADS, ATT_HEAD_GROUP):
            chunk_heads(start, tk, diag_offset, range(first, first + ATT_HEAD_GROUP))

    def chunk_heads(start, tk, diag_offset, group):
        st = {h: lax.dot_general(k_ref[0, pl.ds(start, tk), lanes_of(h)], qs_ref[h], (((1,), (1,)), ((), ())),
                                 preferred_element_type=F32) for h in group}
        if diag_offset is not None:
            key = lax.broadcasted_iota(jnp.int32, (tk, 2 * tq), 0)
            qry = lax.broadcasted_iota(jnp.int32, (tk, 2 * tq), 1) % tq + diag_offset
            st = {h: jnp.where(key <= qry, s, -1e30) for h, s in st.items()}
        m_old = {h: m_ref[h] for h in group}
        m_new = {h: jnp.maximum(m_old[h], jnp.max(st[h], 0, keepdims=True)) for h in group}
        pt = {h: jnp.exp2(st[h] - m_new[h]) for h in group}
        corr = {h: jnp.exp2(m_old[h] - m_new[h]) for h in group}
        pv = {h: lax.dot_general(v_ref[0, pl.ds(start, tk), lanes_of(h)], pt[h].astype(BF16),
                                 (((0,), (0,)), ((), ())), preferred_element_type=F32)
              for h in group}
        for h in group:
            m_ref[h] = m_new[h]
            l_ref[h] = l_ref[h] * corr[h] + jnp.sum(pt[h], 0, keepdims=True)
            acc_ref[h] = acc_ref[h] * corr[h] + pv[h]

    @pl.loop(0, qi // 2)
    def _(j):
        chunk(pl.multiple_of(j * (2 * tq), 2 * tq), 2 * tq, None)

    @pl.when(qi % 2 == 1)
    def _():
        chunk(pl.multiple_of((qi - 1) * tq, tq), 2 * tq, tq)

    @pl.when(qi % 2 == 0)
    def _():
        chunk(pl.multiple_of(qi * tq, tq), tq, 0)

    for h in heads:
        acc = acc_ref[h]
        l = l_ref[h]
        o = acc[:, :tq] / l[:, :tq] - lam * (acc[:, tq:] / l[:, tq:])
        ms = jnp.mean(o * o, 0, keepdims=True)
        o = o * lax.rsqrt(ms + SUBLN_EPS) * g_ref[...]
        o_ref[0, :, lanes_of(h)] = (o * (1.0 - lam_init)).T.astype(BF16)


def _diff_attn(q, k, v, lq1, lk1, lq2, lk2, subln_g, lam_init):
    B, T, W = q.shape
    tq = TQ_ATT
    vec = lambda a: a.reshape(1, -1)
    vspec = lambda n: pl.BlockSpec((1, n), lambda b, i: (0, 0))
    return pl.pallas_call(
        functools.partial(_diff_attn_kernel, tq=tq, lam_init=lam_init),
        grid=(B, T // tq),
        in_specs=[pl.BlockSpec((1, tq, W), lambda b, i: (b, i, 0)),
                  pl.BlockSpec((1, T, W), lambda b, i: (b, 0, 0)),
                  pl.BlockSpec((1, T, W), lambda b, i: (b, 0, 0)),
                  vspec(DIFF_HEAD_DIM), vspec(DIFF_HEAD_DIM), vspec(DIFF_HEAD_DIM), vspec(DIFF_HEAD_DIM),
                  pl.BlockSpec((2 * DIFF_HEAD_DIM, 1), lambda b, i: (0, 0))],
        out_specs=pl.BlockSpec((1, tq, W), lambda b, i: (b, i, 0)),
        out_shape=jax.ShapeDtypeStruct((B, T, W), BF16),
        scratch_shapes=[pltpu.VMEM((DIFF_HEADS, 2 * tq, LANES), BF16),
                        pltpu.VMEM((DIFF_HEADS, 1, 2 * tq), F32),
                        pltpu.VMEM((DIFF_HEADS, 1, 2 * tq), F32),
                        pltpu.VMEM((DIFF_HEADS, LANES, 2 * tq), F32)],
        compiler_params=_cparams(("parallel", "parallel")),
        name="diff_attention",
    )(q, k, v, vec(lq1), vec(lk1), vec(lq2), vec(lk2), subln_g.reshape(-1, 1))


def _sigmoid(z):
    return 1.0 / (1.0 + jnp.exp(-z))


def _rwkv_proj_kernel(*refs, tm, has_vres):
    if has_vres:
        (x_ref, xh_ref, mix_ref, wr_ref, wk_ref, wv_ref, w0_ref, w1_ref, w2_ref, a0_ref, a1_ref, a2_ref,
         g1_ref, g2_ref, v0_ref, v1_ref, v2_ref, vf_ref,
         r_ref, k_ref, v_ref, lw_ref, a_ref, g_ref, xbuf) = refs
    else:
        (x_ref, xh_ref, mix_ref, wr_ref, wk_ref, wv_ref, w0_ref, w1_ref, w2_ref, a0_ref, a1_ref, a2_ref,
         g1_ref, g2_ref,
         r_ref, k_ref, v_ref, lw_ref, a_ref, g_ref, xbuf) = refs
    i = pl.program_id(1)
    x = x_ref[0]
    xbuf[0:SUBLANES, :] = xh_ref[0] * (i > 0).astype(F32)
    xbuf[SUBLANES:, :] = x
    xx = xbuf[pl.ds(SUBLANES - 1, tm), :] - x
    mix = mix_ref[...]

    def mixed(n):
        return (x + xx * mix[n:n + 1]).astype(BF16)

    r_ref[0] = jnp.dot(mixed(0), wr_ref[...], preferred_element_type=F32)
    xw = mixed(1)
    wpre = w0_ref[...] + _bdot(jnp.tanh(jnp.dot(xw, w1_ref[...], preferred_element_type=F32)), w2_ref[...])
    w_log = jnp.minimum(wpre, 0.0) - jnp.log(1.0 + jnp.exp(-jnp.abs(wpre))) - 0.5
    lw_ref[0] = -jnp.exp(w_log)
    k_ref[0] = jnp.dot(mixed(2), wk_ref[...], preferred_element_type=F32)
    xv = mixed(3)
    v = jnp.dot(xv, wv_ref[...], preferred_element_type=F32)
    if has_vres:
        gate = _sigmoid(v0_ref[...] + _bdot(jnp.dot(xv, v1_ref[...], preferred_element_type=F32), v2_ref[...]))
        v = v + (vf_ref[0] - v) * gate
    v_ref[0] = v
    xa = mixed(4)
    a_ref[0] = _sigmoid(a0_ref[...] + _bdot(jnp.dot(xa, a1_ref[...], preferred_element_type=F32), a2_ref[...]))
    xg = mixed(5)
    g_ref[0] = _bdot(_sigmoid(jnp.dot(xg, g1_ref[...], preferred_element_type=F32)), g2_ref[...]).astype(BF16)


def _rwkv_proj(x, mix, w_r, w_k, w_v, w0, w1, w2, a0, a1, a2, g1, g2, vres, v_first):
    B, T, D = x.shape
    tm = TM_RWKV
    halo = _halo_index(tm, SUBLANES)
    has_vres = vres is not None
    full = lambda a: pl.BlockSpec(a.shape, lambda b, i: (0,) * a.ndim, pipeline_mode=pl.Buffered(1))
    tile = pl.BlockSpec((1, tm, D), lambda b, i: (b, i, 0))
    row = lambda a: a.reshape(1, D)
    args = [x, x, mix, w_r, w_k, w_v, row(w0), w1, w2, row(a0), a1, a2, g1, g2]
    if has_vres:
        v0, v1, v2 = vres
        args += [row(v0), v1, v2, v_first]
    in_specs = ([tile, pl.BlockSpec((1, SUBLANES, D), lambda b, i: (b, halo(i), 0))]
                + [full(a) for a in args[2:14]])
    if has_vres:
        in_specs += [full(args[14]), full(args[15]), full(args[16]), tile]
    f32o = jax.ShapeDtypeStruct((B, T, D), F32)
    return pl.pallas_call(
        functools.partial(_rwkv_proj_kernel, tm=tm, has_vres=has_vres),
        grid=(B, T // tm),
        in_specs=in_specs,
        out_specs=[tile] * 6,
        out_shape=[f32o, f32o, f32o, f32o, f32o, jax.ShapeDtypeStruct((B, T, D), BF16)],
        scratch_shapes=[pltpu.VMEM((tm + SUBLANES, D), F32)],
        compiler_params=_cparams(("parallel", "parallel")),
        name="rwkv_proj",
    )(*args)


def _stack_heads(z, head0):
    zero = jnp.zeros_like(z)
    return jnp.concatenate([jnp.where(head0, z, zero), jnp.where(head0, zero, z)], axis=0)


def _head_sum(z, head0):
    zero = jnp.zeros_like(z)
    s0 = jnp.sum(jnp.where(head0, z, zero), -1, keepdims=True)
    s1 = jnp.sum(jnp.where(head0, zero, z), -1, keepdims=True)
    return jnp.where(head0, s0, s1)


def _unit_lower_inverse(Ls, row, col, n_time):
    zero = jnp.zeros_like(Ls[0])
    eye = (row == col).astype(F32)
    base = (row // 2) == (col // 2)
    ts = [eye + jnp.where(base, L, zero) for L in Ls]
    s = 2
    while s < n_time:
        lower_left = ((row // (2 * s)) == (col // (2 * s))) & ((row // s) % 2 == 1) & ((col // s) % 2 == 0)
        offs = [_bdot(t, jnp.where(lower_left, L, zero)) for t, L in zip(ts, Ls)]
        ts = [t + _bdot(o, t) for t, o in zip(ts, offs)]
        s *= 2
    return ts


def _wkv_kernel(r_ref, k_ref, v_ref, lw_ref, a_ref, g_ref, kk_ref, ka_ref, rk_ref, gng_ref, gnb_ref,
                o_ref, st_ref, *, C, n_pairs):
    @pl.when(pl.program_id(1) == 0)
    def _():
        st_ref[...] = jnp.zeros_like(st_ref)

    P = 2 * C
    lane = lax.broadcasted_iota(jnp.int32, (1, LANES), 1)
    head0 = lane < RWKV_HEAD
    row = lax.broadcasted_iota(jnp.int32, (P, P), 0)
    col = lax.broadcasted_iota(jnp.int32, (P, P), 1)
    same_head = (row // C) == (col // C)
    strict = same_head & (col < row)
    incl = same_head & (col <= row)
    trow = lax.broadcasted_iota(jnp.int32, (C, LANES), 0)
    zero_pp = jnp.zeros((P, P), F32)
    pairs = range(o_ref.shape[0] * n_pairs)
    lanes_of = lambda p: slice((p % n_pairs) * LANES, (p % n_pairs + 1) * LANES)

    def prepare(p):
        sl = lanes_of(p)
        r = r_ref[p // n_pairs, :, sl]
        k = k_ref[p // n_pairs, :, sl]
        lw = lw_ref[p // n_pairs, :, sl]
        a = a_ref[p // n_pairs, :, sl]
        kk = k * kk_ref[:, sl]
        norm = jnp.sqrt(_head_sum(kk * kk, head0))
        kk = kk / jnp.maximum(norm, 1e-12)
        k2 = k * (1.0 + (a - 1.0) * ka_ref[:, sl])
        cum = lw
        sh = 1
        while sh < C:
            cum = cum + jnp.where(trow >= sh, pltpu.roll(cum, sh, 0), 0.0)
            sh *= 2
        cum_end = cum[C - 1:C, :]
        e_pos = jnp.exp(cum)
        e_neg = jnp.exp(-cum)
        e_tail = jnp.exp(cum_end - cum)
        b_t = kk * a
        return dict(
            a_s=_stack_heads(-kk * jnp.exp(cum - lw), head0),
            b_s=_stack_heads(b_t * e_neg, head0),
            k_s=_stack_heads(k2 * e_neg, head0),
            r_s=_stack_heads(r * e_pos, head0),
            bh_s=_stack_heads(b_t * e_tail, head0),
            kh_s=_stack_heads(k2 * e_tail, head0),
            v_s=_stack_heads(v_ref[p // n_pairs, :, sl], head0),
            decay_end=jnp.exp(cum_end))

    d = [prepare(p) for p in pairs]
    grams = [_bdot_nt(jnp.concatenate([d[p]["a_s"], d[p]["r_s"]], 0),
                      jnp.concatenate([d[p]["b_s"], d[p]["k_s"]], 0)) for p in pairs]
    l_ab = [jnp.where(strict, g[:P, :P], zero_pp) for g in grams]
    lv = [_bdot(jnp.where(strict, grams[p][:P, P:], zero_pp), d[p]["v_s"]) for p in pairs]
    m_r = [jnp.concatenate([jnp.where(incl, g[P:, :P], zero_pp), jnp.where(incl, g[P:, P:], zero_pp)], 1)
           for g in grams]
    t_inv = _unit_lower_inverse(l_ab, row, col, C)
    ty = [_bdot(t_inv[p], jnp.concatenate([d[p]["a_s"], lv[p]], 1)) for p in pairs]
    sts = [st_ref[p] for p in pairs]
    zs = [_bdot_nt(jnp.concatenate([ty[p][:, :LANES], d[p]["r_s"]], 0), sts[p]) for p in pairs]
    uvs = [jnp.concatenate([zs[p][:P] + ty[p][:, LANES:], d[p]["v_s"]], 0) for p in pairs]
    o_s = [zs[p][P:] + _bdot(m_r[p], uvs[p]) for p in pairs]
    for p in pairs:
        st_ref[p] = sts[p] * d[p]["decay_end"] + _bdot(uvs[p].T, jnp.concatenate([d[p]["bh_s"], d[p]["kh_s"]], 0))

    for p in pairs:
        sl = lanes_of(p)
        o = o_s[p][:C] + o_s[p][C:]
        r = r_ref[p // n_pairs, :, sl]
        k2 = k_ref[p // n_pairs, :, sl] * (1.0 + (a_ref[p // n_pairs, :, sl] - 1.0) * ka_ref[:, sl])
        mu = _head_sum(o, head0) * (1.0 / RWKV_HEAD)
        oc = o - mu
        var = _head_sum(oc * oc, head0) * (1.0 / RWKV_HEAD)
        on = oc * lax.rsqrt(var + GN_EPS) * gng_ref[:, sl] + gnb_ref[:, sl]
        bonus = _head_sum(r * k2 * rk_ref[:, sl], head0) * v_ref[p // n_pairs, :, sl]
        o_ref[p // n_pairs, :, sl] = ((on + bonus) * g_ref[p // n_pairs, :, sl].astype(F32)).astype(BF16)


def _wkv(r, k, v, lw, a, g, k_k, k_a, r_k, gn_g, gn_b):
    B, T, D = r.shape
    C = WKV_CHUNK
    n_pairs = D // LANES
    nb = WKV_BATCH
    tile = pl.BlockSpec((nb, C, D), lambda b, c: (b, c, 0))
    vspec = pl.BlockSpec((1, D), lambda b, c: (0, 0))
    row = lambda z: z.reshape(1, D)
    return pl.pallas_call(
        functools.partial(_wkv_kernel, C=C, n_pairs=n_pairs),
        grid=(B // nb, T // C),
        in_specs=[tile] * 6 + [vspec] * 5,
        out_specs=tile,
        out_shape=jax.ShapeDtypeStruct((B, T, D), BF16),
        scratch_shapes=[pltpu.VMEM((nb * n_pairs, LANES, LANES), F32)],
        compiler_params=_cparams(("parallel", "arbitrary")),
        name="wkv7",
    )(r, k, v, lw, a, g, row(k_k), row(k_a), row(r_k), row(gn_g), row(gn_b))


def _ffn_kernel(x_ref, xh_ref, wup_ref, cw_ref, cb_ref, wd_ref, g_ref, b_ref,
                o_ref, xcat_ref, hg_ref, hu_ref, act_ref, *, tm, fc, d_ff):
    i = pl.program_id(1)
    x = x_ref[0]
    xcat_ref[0:HALO, :] = (xh_ref[0] * (i > 0).astype(F32)).astype(BF16)
    xcat_ref[HALO:, :] = x.astype(BF16)

    n_chunks = d_ff // fc

    def up_project(c):
        for col, h_ref in ((c * fc, hg_ref), (d_ff + c * fc, hu_ref)):
            h_ref[c % 2] = jnp.dot(xcat_ref[...], wup_ref[:, col:col + fc], preferred_element_type=F32)

    def conv(col, h_ref):
        cw = cw_ref[:, col:col + fc]
        return (cw[2:3] * h_ref[pl.ds(HALO, tm), :] + cw[1:2] * h_ref[pl.ds(HALO - 1, tm), :]
                + cw[0:1] * h_ref[pl.ds(HALO - 2, tm), :] + cb_ref[:, col:col + fc])

    up_project(0)
    y = None
    group_start = 0
    for c in range(n_chunks):
        if c + 1 < n_chunks:
            up_project(c + 1)
        gate = conv(c * fc, hg_ref.at[c % 2])
        up = conv(d_ff + c * fc, hu_ref.at[c % 2])
        half = 0.5 * gate
        act_ref[:, c * fc:(c + 1) * fc] = (half * (1.0 + jnp.tanh(half)) * up).astype(BF16)
        if (c + 1) % FFN_DOWN_GROUP == 0 or c + 1 == n_chunks:
            lo, hi = group_start * fc, (c + 1) * fc
            part = jnp.dot(act_ref[:, lo:hi], wd_ref[lo:hi, :], preferred_element_type=F32)
            y = part if y is None else y + part
            group_start = c + 1
    o_ref[0] = _layer_norm_rows(DN_ALPHA * x + y, g_ref[...], b_ref[...])


def _ffn(x, w_up, conv_w, conv_b, w_down, ln_g, ln_b):
    B, T, D = x.shape
    d_ff = w_down.shape[0]
    tm, fc = TM_FFN, FC_FFN
    assert d_ff % fc == 0
    halo = _halo_index(tm, HALO)
    const = lambda a: pl.BlockSpec(a.shape, lambda b, i: (0,) * a.ndim, pipeline_mode=pl.Buffered(1))
    conv_b = conv_b.reshape(1, 2 * d_ff)
    ln_g = ln_g.reshape(1, D)
    ln_b = ln_b.reshape(1, D)
    return pl.pallas_call(
        functools.partial(_ffn_kernel, tm=tm, fc=fc, d_ff=d_ff),
        grid=(B, T // tm),
        in_specs=[pl.BlockSpec((1, tm, D), lambda b, i: (b, i, 0)),
                  pl.BlockSpec((1, HALO, D), lambda b, i: (b, halo(i), 0)),
                  const(w_up), const(conv_w), const(conv_b), const(w_down), const(ln_g), const(ln_b)],
        out_specs=pl.BlockSpec((1, tm, D), lambda b, i: (b, i, 0)),
        out_shape=jax.ShapeDtypeStruct((B, T, D), F32),
        scratch_shapes=[pltpu.VMEM((tm + HALO, D), BF16),
                        pltpu.VMEM((2, tm + HALO, fc), F32), pltpu.VMEM((2, tm + HALO, fc), F32),
                        pltpu.VMEM((tm, d_ff), BF16)],
        compiler_params=_cparams(("parallel", "parallel")),
        name="conv_glu_ffn",
    )(x, x, w_up, conv_w, conv_b, w_down, ln_g, ln_b)


def _rope_tables(T):
    half = DIFF_HEAD_DIM // 2
    inv = 1.0 / (ROPE_THETA ** (jnp.arange(0, DIFF_HEAD_DIM, 2, dtype=F32) / DIFF_HEAD_DIM))
    ang = jnp.arange(T, dtype=F32)[:, None] * inv[None, :]
    cos, sin = jnp.cos(ang), jnp.sin(ang)
    reps = LANES // DIFF_HEAD_DIM
    cos_t = jnp.tile(jnp.concatenate([cos, cos], -1), (1, reps))
    sin_t = jnp.tile(jnp.concatenate([-sin, sin], -1), (1, reps))
    assert cos_t.shape == (T, LANES) and half * 2 == DIFF_HEAD_DIM
    return cos_t, sin_t


def _even_layer(x, w_in, conv_w, lq1, lk1, lq2, lk2, subln_g, w_out, lam_init, cos_t, sin_t, ln_g, ln_b):
    B, T, D = x.shape
    W = CONV_WIDTH_MIX
    y_a, q, k, v = _even_in(x, w_in.astype(BF16), conv_w, cos_t, sin_t)
    y_b = _diff_attn(q, k, v, lq1, lk1, lq2, lk2, subln_g, lam_init)
    w_out = w_out.astype(BF16)
    out = _proj_res_ln([y_a.reshape(B * T, W), y_b.reshape(B * T, W)], [w_out[:W], w_out[W:]],
                       x.reshape(B * T, D), ln_g, ln_b)
    return out.reshape(B, T, D)


def _odd_layer(x, mix, w_r, w_k, w_v, w_o, w0, w1, w2, a0, a1, a2, g1, g2, k_k, k_a, r_k, gn_g, gn_b,
               v_first, vres, ln_g, ln_b):
    B, T, D = x.shape
    bf = lambda z: z.astype(BF16)
    if vres is not None:
        vres = (vres[0], bf(vres[1]), bf(vres[2]))
    r, k, v, lw, a, g = _rwkv_proj(x, mix, bf(w_r), bf(w_k), bf(w_v), w0, bf(w1), bf(w2), a0, bf(a1), bf(a2),
                                   bf(g1), bf(g2), vres, v_first)
    if vres is None:
        v_first = v
    o = _wkv(r, k, v, lw, a, g, k_k, k_a, r_k.reshape(-1), gn_g, gn_b)
    out = _proj_res_ln([o.reshape(B * T, D)], [bf(w_o)], x.reshape(B * T, D), ln_g, ln_b)
    return out.reshape(B, T, D), v_first


def kernel(x, ev_w_in, ev_conv_w, ev_lam_q1, ev_lam_k1, ev_lam_q2, ev_lam_k2, ev_subln_g, ev_w_out, rw_mix, rw_w_r, rw_w_k, rw_w_v, rw_w_o, rw_w0, rw_w1, rw_w2, rw_a0, rw_a1, rw_a2, rw_g1, rw_g2, rw_k_k, rw_k_a, rw_r_k, rw_gn_g, rw_gn_b, rw_v0, rw_v1, rw_v2, ffn_w_up, ffn_conv_w, ffn_conv_b, ffn_w_down, ln1_g, ln1_b, ln2_g, ln2_b):
    T = x.shape[1]
    cos_t, sin_t = _rope_tables(T)
    v_first = None
    for l in range(DEPTH):
        if l % 2 == 0:
            i = l // 2
            lam_init = 0.8 - 0.6 * math.exp(-0.3 * l)
            x = _even_layer(x, ev_w_in[i], ev_conv_w[i], ev_lam_q1[i], ev_lam_k1[i], ev_lam_q2[i],
                            ev_lam_k2[i], ev_subln_g[i], ev_w_out[i], lam_init, cos_t, sin_t,
                            ln1_g[l], ln1_b[l])
        else:
            j = l // 2
            vres = None if j == 0 else (rw_v0[j - 1], rw_v1[j - 1], rw_v2[j - 1])
            x, v_first = _odd_layer(x, rw_mix[j], rw_w_r[j], rw_w_k[j], rw_w_v[j], rw_w_o[j], rw_w0[j],
                                    rw_w1[j], rw_w2[j], rw_a0[j], rw_a1[j], rw_a2[j], rw_g1[j], rw_g2[j],
                                    rw_k_k[j], rw_k_a[j], rw_r_k[j], rw_gn_g[j], rw_gn_b[j],
                                    v_first, vres, ln1_g[l], ln1_b[l])
        x = _ffn(x, ffn_w_up[l].astype(BF16), ffn_conv_w[l], ffn_conv_b[l], ffn_w_down[l].astype(BF16),
                 ln2_g[l], ln2_b[l])
    return x
```

```python
import functools
import math

import jax
import jax.numpy as jnp
from jax import lax
from jax.experimental import pallas as pl
from jax.experimental.pallas import tpu as pltpu

F32 = jnp.float32
BF16 = jnp.bfloat16

DEPTH = 4
DN_ALPHA = (2 * DEPTH) ** 0.25
LN_EPS = 1e-5
CONV_WIDTH_MIX = 512
DIFF_HEADS = 4
DIFF_HEAD_DIM = 64
ROPE_THETA = 10000.0
SUBLN_EPS = 1e-5
LOG2_E = math.log2(math.e)
RWKV_HEAD = 64
GN_EPS = 64e-5

LANES = 128
SUBLANES = 8
HALO = 16
VMEM_LIMIT_BYTES = 56 * 1024 * 1024

TM_PROJ = 512
TM_EVEN = 512
TQ_ATT = 256
ATT_HEAD_GROUP = 4
TM_RWKV = 512
WKV_CHUNK = 64
WKV_BATCH = 2
TM_FFN = 512
FC_FFN = 256
FFN_DOWN_GROUP = 4


def _cparams(semantics):
    return pltpu.CompilerParams(dimension_semantics=semantics, vmem_limit_bytes=VMEM_LIMIT_BYTES)


def _bdot(a, b):
    return jnp.dot(a.astype(BF16), b.astype(BF16), preferred_element_type=F32)


def _bdot_nt(a, b):
    return lax.dot_general(a.astype(BF16), b.astype(BF16), (((1,), (1,)), ((), ())),
                           preferred_element_type=F32)


def _layer_norm_rows(z, g, b):
    mu = jnp.mean(z, -1, keepdims=True)
    zc = z - mu
    var = jnp.mean(zc * zc, -1, keepdims=True)
    return zc * lax.rsqrt(var + LN_EPS) * g + b


def _halo_index(tm, rows):
    per = tm // rows
    return lambda i: jnp.maximum(i * per - 1, 0)


def _proj_res_ln_kernel(*refs, n_pairs):
    a_refs = refs[:n_pairs]
    w_refs = refs[n_pairs:2 * n_pairs]
    x_ref, g_ref, b_ref, o_ref = refs[2 * n_pairs:]
    y = None
    for a_ref, w_ref in zip(a_refs, w_refs):
        d = jnp.dot(a_ref[...], w_ref[...], preferred_element_type=F32)
        y = d if y is None else y + d
    z = DN_ALPHA * x_ref[...] + y
    o_ref[...] = _layer_norm_rows(z, g_ref[...], b_ref[...])


def _proj_res_ln(a_list, w_list, x, g, b):
    M, D = x.shape
    tm = TM_PROJ
    n = len(a_list)
    in_specs = ([pl.BlockSpec((tm, a.shape[1]), lambda i: (i, 0)) for a in a_list]
                + [pl.BlockSpec(w.shape, lambda i: (0, 0)) for w in w_list]
                + [pl.BlockSpec((tm, D), lambda i: (i, 0)),
                   pl.BlockSpec((1, D), lambda i: (0, 0)),
                   pl.BlockSpec((1, D), lambda i: (0, 0))])
    return pl.pallas_call(
        functools.partial(_proj_res_ln_kernel, n_pairs=n),
        grid=(M // tm,),
        in_specs=in_specs,
        out_specs=pl.BlockSpec((tm, D), lambda i: (i, 0)),
        out_shape=jax.ShapeDtypeStruct((M, D), F32),
        compiler_params=_cparams(("parallel",)),
        name="proj_res_ln",
    )(*a_list, *w_list, x, g.reshape(1, D), b.reshape(1, D))


def _rope_block(z, c, s, first_half):
    half = DIFF_HEAD_DIM // 2
    partner = jnp.where(first_half, pltpu.roll(z, LANES - half, 1), pltpu.roll(z, half, 1))
    return z * c + partner * s


def _even_in_kernel(x_ref, xh_ref, w_ref, cw_ref, cos_ref, sin_ref,
                    ya_ref, q_ref, k_ref, v_ref, xcat_ref, ubuf, *, tm):
    i = pl.program_id(1)
    W = CONV_WIDTH_MIX
    xcat_ref[0:HALO, :] = (xh_ref[0] * (i > 0).astype(F32)).astype(BF16)
    xcat_ref[HALO:, :] = x_ref[0].astype(BF16)
    xb = xcat_ref[HALO:, :]
    gcx = jnp.dot(xcat_ref[...], w_ref[:, W:3 * W], preferred_element_type=F32)
    ubuf[...] = gcx[:, :W] * gcx[:, W:]
    gb = jnp.dot(xb, w_ref[:, 0:W], preferred_element_type=F32)
    cw = cw_ref[...]
    conv = (cw[2:3] * ubuf[pl.ds(HALO, tm), :] + cw[1:2] * ubuf[pl.ds(HALO - 1, tm), :]
            + cw[0:1] * ubuf[pl.ds(HALO - 2, tm), :])
    ya_ref[0] = (gb * conv).astype(BF16)

    lane = lax.broadcasted_iota(jnp.int32, (1, LANES), 1)
    first_half = (lane % DIFF_HEAD_DIM) < (DIFF_HEAD_DIM // 2)
    c = cos_ref[...]
    s = sin_ref[...]
    scale = DIFF_HEAD_DIM ** -0.5 * LOG2_E
    q = jnp.dot(xb, w_ref[:, 3 * W:4 * W], preferred_element_type=F32)
    k = jnp.dot(xb, w_ref[:, 4 * W:5 * W], preferred_element_type=F32)
    for blk in range(W // LANES):
        sl = slice(blk * LANES, (blk + 1) * LANES)
        q_ref[0, :, sl] = (_rope_block(q[:, sl], c, s, first_half) * scale).astype(BF16)
        k_ref[0, :, sl] = _rope_block(k[:, sl], c, s, first_half).astype(BF16)
    v_ref[0] = jnp.dot(xb, w_ref[:, 5 * W:6 * W], preferred_element_type=F32).astype(BF16)


def _even_in(x, w_in, conv_w, cos_t, sin_t):
    B, T, D = x.shape
    tm = TM_EVEN
    W = CONV_WIDTH_MIX
    halo = _halo_index(tm, HALO)
    out = jax.ShapeDtypeStruct((B, T, W), BF16)
    ospec = pl.BlockSpec((1, tm, W), lambda b, i: (b, i, 0))
    return pl.pallas_call(
        functools.partial(_even_in_kernel, tm=tm),
        grid=(B, T // tm),
        in_specs=[pl.BlockSpec((1, tm, D), lambda b, i: (b, i, 0)),
                  pl.BlockSpec((1, HALO, D), lambda b, i: (b, halo(i), 0)),
                  pl.BlockSpec(w_in.shape, lambda b, i: (0, 0), pipeline_mode=pl.Buffered(1)),
                  pl.BlockSpec(conv_w.shape, lambda b, i: (0, 0)),
                  pl.BlockSpec((tm, LANES), lambda b, i: (i, 0)),
                  pl.BlockSpec((tm, LANES), lambda b, i: (i, 0))],
        out_specs=[ospec, ospec, ospec, ospec],
        out_shape=[out, out, out, out],
        scratch_shapes=[pltpu.VMEM((tm + HALO, D), BF16), pltpu.VMEM((tm + HALO, W), F32)],
        compiler_params=_cparams(("parallel", "parallel")),
        name="even_in_proj",
    )(x, x, w_in, conv_w, cos_t, sin_t)


def _diff_attn_kernel(q_ref, k_ref, v_ref, lq1_ref, lk1_ref, lq2_ref, lk2_ref, g_ref, o_ref,
                      qs_ref, m_ref, l_ref, acc_ref, *, tq, lam_init):
    qi = pl.program_id(1)
    heads = range(DIFF_HEADS)
    lanes_of = lambda h: slice(h * LANES, (h + 1) * LANES)
    lam = (jnp.exp(jnp.sum(lq1_ref[...] * lk1_ref[...], keepdims=True))
           - jnp.exp(jnp.sum(lq2_ref[...] * lk2_ref[...], keepdims=True)) + lam_init)
    lane = lax.broadcasted_iota(jnp.int32, (1, LANES), 1)
    map0 = lane < DIFF_HEAD_DIM
    for h in heads:
        q = q_ref[0, :, lanes_of(h)]
        zero = jnp.zeros_like(q)
        qs_ref[h] = jnp.concatenate([jnp.where(map0, q, zero), jnp.where(map0, zero, q)], axis=0)
    m_ref[...] = jnp.full(m_ref.shape, -1e30, F32)
    l_ref[...] = jnp.zeros(l_ref.shape, F32)
    acc_ref[...] = jnp.zeros(acc_ref.shape, F32)

    def chunk(start, tk, diag_offset):
        for first in range(0, DIFF_HEADS, ATT_HEAD_GROUP):
            chunk_heads(start, tk, diag_offset, range(first, first + ATT_HEAD_GROUP))

    def chunk_heads(start, tk, diag_offset, group):
        st = {h: lax.dot_general(k_ref[0, pl.ds(start, tk), lanes_of(h)], qs_ref[h], (((1,), (1,)), ((), ())),
                                 preferred_element_type=F32) for h in group}
        if diag_offset is not None:
            key = lax.broadcasted_iota(jnp.int32, (tk, 2 * tq), 0)
            qry = lax.broadcasted_iota(jnp.int32, (tk, 2 * tq), 1) % tq + diag_offset
            st = {h: jnp.where(key <= qry, s, -1e30) for h, s in st.items()}
        m_old = {h: m_ref[h] for h in group}
        m_new = {h: jnp.maximum(m_old[h], jnp.max(st[h], 0, keepdims=True)) for h in group}
        pt = {h: jnp.exp2(st[h] - m_new[h]) for h in group}
        corr = {h: jnp.exp2(m_old[h] - m_new[h]) for h in group}
        pv = {h: lax.dot_general(v_ref[0, pl.ds(start, tk), lanes_of(h)], pt[h].astype(BF16),
                                 (((0,), (0,)), ((), ())), preferred_element_type=F32)
              for h in group}
        for h in group:
            m_ref[h] = m_new[h]
            l_ref[h] = l_ref[h] * corr[h] + jnp.sum(pt[h], 0, keepdims=True)
            acc_ref[h] = acc_ref[h] * corr[h] + pv[h]

    @pl.loop(0, qi // 2)
    def _(j):
        chunk(pl.multiple_of(j * (2 * tq), 2 * tq), 2 * tq, None)

    @pl.when(qi % 2 == 1)
    def _():
        chunk(pl.multiple_of((qi - 1) * tq, tq), 2 * tq, tq)

    @pl.when(qi % 2 == 0)
    def _():
        chunk(pl.multiple_of(qi * tq, tq), tq, 0)

    for h in heads:
        acc = acc_ref[h]
        inv_l = 1.0 / l_ref[h]
        o = acc[:, :tq] * inv_l[:, :tq] - (lam * inv_l[:, tq:]) * acc[:, tq:]
        ms = jnp.mean(o * o, 0, keepdims=True)
        o = o * lax.rsqrt(ms + SUBLN_EPS) * g_ref[...]
        o_ref[0, :, lanes_of(h)] = (o * (1.0 - lam_init)).T.astype(BF16)


def _diff_attn(q, k, v, lq1, lk1, lq2, lk2, subln_g, lam_init):
    B, T, W = q.shape
    tq = TQ_ATT
    vec = lambda a: a.reshape(1, -1)
    vspec = lambda n: pl.BlockSpec((1, n), lambda b, i: (0, 0))
    return pl.pallas_call(
        functools.partial(_diff_attn_kernel, tq=tq, lam_init=lam_init),
        grid=(B, T // tq),
        in_specs=[pl.BlockSpec((1, tq, W), lambda b, i: (b, i, 0)),
                  pl.BlockSpec((1, T, W), lambda b, i: (b, 0, 0)),
                  pl.BlockSpec((1, T, W), lambda b, i: (b, 0, 0)),
                  vspec(DIFF_HEAD_DIM), vspec(DIFF_HEAD_DIM), vspec(DIFF_HEAD_DIM), vspec(DIFF_HEAD_DIM),
                  pl.BlockSpec((2 * DIFF_HEAD_DIM, 1), lambda b, i: (0, 0))],
        out_specs=pl.BlockSpec((1, tq, W), lambda b, i: (b, i, 0)),
        out_shape=jax.ShapeDtypeStruct((B, T, W), BF16),
        scratch_shapes=[pltpu.VMEM((DIFF_HEADS, 2 * tq, LANES), BF16),
                        pltpu.VMEM((DIFF_HEADS, 1, 2 * tq), F32),
                        pltpu.VMEM((DIFF_HEADS, 1, 2 * tq), F32),
                        pltpu.VMEM((DIFF_HEADS, LANES, 2 * tq), F32)],
        compiler_params=_cparams(("parallel", "parallel")),
        name="diff_attention",
    )(q, k, v, vec(lq1), vec(lk1), vec(lq2), vec(lk2), subln_g.reshape(-1, 1))


def _sigmoid(z):
    return 0.5 + 0.5 * jnp.tanh(0.5 * z)


def _rwkv_proj_kernel(*refs, tm, has_vres):
    if has_vres:
        (x_ref, xh_ref, mix_ref, wr_ref, wk_ref, wv_ref, w0_ref, w1_ref, w2_ref, a0_ref, a1_ref, a2_ref,
         g1_ref, g2_ref, v0_ref, v1_ref, v2_ref, vf_ref,
         r_ref, k_ref, v_ref, lw_ref, a_ref, g_ref, xbuf) = refs
    else:
        (x_ref, xh_ref, mix_ref, wr_ref, wk_ref, wv_ref, w0_ref, w1_ref, w2_ref, a0_ref, a1_ref, a2_ref,
         g1_ref, g2_ref,
         r_ref, k_ref, v_ref, lw_ref, a_ref, g_ref, xbuf) = refs
    i = pl.program_id(1)
    x = x_ref[0]
    xbuf[0:SUBLANES, :] = xh_ref[0] * (i > 0).astype(F32)
    xbuf[SUBLANES:, :] = x
    xx = xbuf[pl.ds(SUBLANES - 1, tm), :] - x
    mix = mix_ref[...]

    def mixed(n):
        return (x + xx * mix[n:n + 1]).astype(BF16)

    r_ref[0] = jnp.dot(mixed(0), wr_ref[...], preferred_element_type=F32)
    xw = mixed(1)
    wpre = w0_ref[...] + _bdot(jnp.tanh(jnp.dot(xw, w1_ref[...], preferred_element_type=F32)), w2_ref[...])
    w_log = jnp.minimum(wpre, 0.0) - jnp.log(1.0 + jnp.exp(-jnp.abs(wpre))) - 0.5
    lw_ref[0] = -jnp.exp(w_log)
    k_ref[0] = jnp.dot(mixed(2), wk_ref[...], preferred_element_type=F32)
    xv = mixed(3)
    v = jnp.dot(xv, wv_ref[...], preferred_element_type=F32)
    if has_vres:
        gate = _sigmoid(v0_ref[...] + _bdot(jnp.dot(xv, v1_ref[...], preferred_element_type=F32), v2_ref[...]))
        v = v + (vf_ref[0] - v) * gate
    v_ref[0] = v
    xa = mixed(4)
    a_ref[0] = _sigmoid(a0_ref[...] + _bdot(jnp.dot(xa, a1_ref[...], preferred_element_type=F32), a2_ref[...]))
    xg = mixed(5)
    g_ref[0] = _bdot(_sigmoid(jnp.dot(xg, g1_ref[...], preferred_element_type=F32)), g2_ref[...]).astype(BF16)


def _rwkv_proj(x, mix, w_r, w_k, w_v, w0, w1, w2, a0, a1, a2, g1, g2, vres, v_first):
    B, T, D = x.shape
    tm = TM_RWKV
    halo = _halo_index(tm, SUBLANES)
    has_vres = vres is not None
    full = lambda a: pl.BlockSpec(a.shape, lambda b, i: (0,) * a.ndim, pipeline_mode=pl.Buffered(1))
    tile = pl.BlockSpec((1, tm, D), lambda b, i: (b, i, 0))
    row = lambda a: a.reshape(1, D)
    args = [x, x, mix, w_r, w_k, w_v, row(w0), w1, w2, row(a0), a1, a2, g1, g2]
    if has_vres:
        v0, v1, v2 = vres
        args += [row(v0), v1, v2, v_first]
    in_specs = ([tile, pl.BlockSpec((1, SUBLANES, D), lambda b, i: (b, halo(i), 0))]
                + [full(a) for a in args[2:14]])
    if has_vres:
        in_specs += [full(args[14]), full(args[15]), full(args[16]), tile]
    f32o = jax.ShapeDtypeStruct((B, T, D), F32)
    return pl.pallas_call(
        functools.partial(_rwkv_proj_kernel, tm=tm, has_vres=has_vres),
        grid=(B, T // tm),
        in_specs=in_specs,
        out_specs=[tile] * 6,
        out_shape=[f32o, f32o, f32o, f32o, f32o, jax.ShapeDtypeStruct((B, T, D), BF16)],
        scratch_shapes=[pltpu.VMEM((tm + SUBLANES, D), F32)],
        compiler_params=_cparams(("parallel", "parallel")),
        name="rwkv_proj",
    )(*args)


def _stack_heads(z, head0):
    zero = jnp.zeros_like(z)
    return jnp.concatenate([jnp.where(head0, z, zero), jnp.where(head0, zero, z)], axis=0)


def _head_sum(z, head0):
    zero = jnp.zeros_like(z)
    s0 = jnp.sum(jnp.where(head0, z, zero), -1, keepdims=True)
    s1 = jnp.sum(jnp.where(head0, zero, z), -1, keepdims=True)
    return jnp.where(head0, s0, s1)


def _unit_lower_inverse(Ls, row, col, n_time):
    zero = jnp.zeros_like(Ls[0])
    eye = (row == col).astype(F32)
    base = (row // 2) == (col // 2)
    ts = [eye + jnp.where(base, L, zero) for L in Ls]
    s = 2
    while s < n_time:
        lower_left = ((row // (2 * s)) == (col // (2 * s))) & ((row // s) % 2 == 1) & ((col // s) % 2 == 0)
        offs = [_bdot(t, jnp.where(lower_left, L, zero)) for t, L in zip(ts, Ls)]
        ts = [t + _bdot(o, t) for t, o in zip(ts, offs)]
        s *= 2
    return ts


def _wkv_kernel(r_ref, k_ref, v_ref, lw_ref, a_ref, g_ref, kk_ref, ka_ref, rk_ref, gng_ref, gnb_ref,
                o_ref, st_ref, *, C, n_pairs):
    @pl.when(pl.program_id(1) == 0)
    def _():
        st_ref[...] = jnp.zeros_like(st_ref)

    P = 2 * C
    lane = lax.broadcasted_iota(jnp.int32, (1, LANES), 1)
    head0 = lane < RWKV_HEAD
    row = lax.broadcasted_iota(jnp.int32, (P, P), 0)
    col = lax.broadcasted_iota(jnp.int32, (P, P), 1)
    same_head = (row // C) == (col // C)
    strict = same_head & (col < row)
    incl = same_head & (col <= row)
    trow = lax.broadcasted_iota(jnp.int32, (C, LANES), 0)
    zero_pp = jnp.zeros((P, P), F32)
    pairs = range(o_ref.shape[0] * n_pairs)
    lanes_of = lambda p: slice((p % n_pairs) * LANES, (p % n_pairs + 1) * LANES)

    def prepare(p):
        sl = lanes_of(p)
        r = r_ref[p // n_pairs, :, sl]
        k = k_ref[p // n_pairs, :, sl]
        lw = lw_ref[p // n_pairs, :, sl]
        a = a_ref[p // n_pairs, :, sl]
        kk = k * kk_ref[:, sl]
        norm = jnp.sqrt(_head_sum(kk * kk, head0))
        kk = kk / jnp.maximum(norm, 1e-12)
        k2 = k * (1.0 + (a - 1.0) * ka_ref[:, sl])
        cum = lw
        sh = 1
        while sh < C:
            cum = cum + jnp.where(trow >= sh, pltpu.roll(cum, sh, 0), 0.0)
            sh *= 2
        cum_end = cum[C - 1:C, :]
        e_pos = jnp.exp(cum)
        e_neg = jnp.exp(-cum)
        e_tail = jnp.exp(cum_end - cum)
        b_t = kk * a
        return dict(
            a_s=_stack_heads(-kk * jnp.exp(cum - lw), head0),
            b_s=_stack_heads(b_t * e_neg, head0),
            k_s=_stack_heads(k2 * e_neg, head0),
            r_s=_stack_heads(r * e_pos, head0),
            bh_s=_stack_heads(b_t * e_tail, head0),
            kh_s=_stack_heads(k2 * e_tail, head0),
            v_s=_stack_heads(v_ref[p // n_pairs, :, sl], head0),
            decay_end=jnp.exp(cum_end))

    d, grams = [], []
    for p in pairs:
        d.append(prepare(p))
        grams.append(_bdot_nt(jnp.concatenate([d[p]["a_s"], d[p]["r_s"]], 0),
                              jnp.concatenate([d[p]["b_s"], d[p]["k_s"]], 0)))
    l_ab = [jnp.where(strict, g[:P, :P], zero_pp) for g in grams]
    lv = [_bdot(jnp.where(strict, grams[p][:P, P:], zero_pp), d[p]["v_s"]) for p in pairs]
    m_r = [jnp.concatenate([jnp.where(incl, g[P:, :P], zero_pp), jnp.where(incl, g[P:, P:], zero_pp)], 1)
           for g in grams]
    t_inv = _unit_lower_inverse(l_ab, row, col, C)
    ty = [_bdot(t_inv[p], jnp.concatenate([d[p]["a_s"], lv[p]], 1)) for p in pairs]
    sts = [st_ref[p] for p in pairs]
    zs = [_bdot_nt(jnp.concatenate([ty[p][:, :LANES], d[p]["r_s"]], 0), sts[p]) for p in pairs]
    uvs = [jnp.concatenate([zs[p][:P] + ty[p][:, LANES:], d[p]["v_s"]], 0) for p in pairs]
    o_s = [zs[p][P:] + _bdot(m_r[p], uvs[p]) for p in pairs]
    for p in pairs:
        st_ref[p] = sts[p] * d[p]["decay_end"] + _bdot(uvs[p].T, jnp.concatenate([d[p]["bh_s"], d[p]["kh_s"]], 0))
        sl = lanes_of(p)
        o = o_s[p][:C] + o_s[p][C:]
        r = r_ref[p // n_pairs, :, sl]
        k2 = k_ref[p // n_pairs, :, sl] * (1.0 + (a_ref[p // n_pairs, :, sl] - 1.0) * ka_ref[:, sl])
        mu = _head_sum(o, head0) * (1.0 / RWKV_HEAD)
        oc = o - mu
        var = _head_sum(oc * oc, head0) * (1.0 / RWKV_HEAD)
        on = oc * lax.rsqrt(var + GN_EPS) * gng_ref[:, sl] + gnb_ref[:, sl]
        bonus = _head_sum(r * k2 * rk_ref[:, sl], head0) * v_ref[p // n_pairs, :, sl]
        o_ref[p // n_pairs, :, sl] = ((on + bonus) * g_ref[p // n_pairs, :, sl].astype(F32)).astype(BF16)


def _wkv(r, k, v, lw, a, g, k_k, k_a, r_k, gn_g, gn_b):
    B, T, D = r.shape
    C = WKV_CHUNK
    n_pairs = D // LANES
    nb = WKV_BATCH
    tile = pl.BlockSpec((nb, C, D), lambda b, c: (b, c, 0))
    vspec = pl.BlockSpec((1, D), lambda b, c: (0, 0))
    row = lambda z: z.reshape(1, D)
    return pl.pallas_call(
        functools.partial(_wkv_kernel, C=C, n_pairs=n_pairs),
        grid=(B // nb, T // C),
        in_specs=[tile] * 6 + [vspec] * 5,
        out_specs=tile,
        out_shape=jax.ShapeDtypeStruct((B, T, D), BF16),
        scratch_shapes=[pltpu.VMEM((nb * n_pairs, LANES, LANES), F32)],
        compiler_params=_cparams(("parallel", "arbitrary")),
        name="wkv7",
    )(r, k, v, lw, a, g, row(k_k), row(k_a), row(r_k), row(gn_g), row(gn_b))


def _ffn_kernel(x_ref, xh_ref, wup_ref, cw_ref, cb_ref, wd_ref, g_ref, b_ref,
                o_ref, xcat_ref, hg_ref, hu_ref, act_ref, *, tm, fc, d_ff):
    i = pl.program_id(1)
    x = x_ref[0]
    xcat_ref[0:HALO, :] = (xh_ref[0] * (i > 0).astype(F32)).astype(BF16)
    xcat_ref[HALO:, :] = x.astype(BF16)

    n_chunks = d_ff // fc

    def up_project(c):
        for col, h_ref in ((c * fc, hg_ref), (d_ff + c * fc, hu_ref)):
            h_ref[c % 2] = jnp.dot(xcat_ref[...], wup_ref[:, col:col + fc], preferred_element_type=F32)

    def conv(col, h_ref):
        cw = cw_ref[:, col:col + fc]
        return (cw[2:3] * h_ref[pl.ds(HALO, tm), :] + cw[1:2] * h_ref[pl.ds(HALO - 1, tm), :]
                + cw[0:1] * h_ref[pl.ds(HALO - 2, tm), :] + cb_ref[:, col:col + fc])

    up_project(0)
    y = None
    group_start = 0
    for c in range(n_chunks):
        if c + 1 < n_chunks:
            up_project(c + 1)
        gate = conv(c * fc, hg_ref.at[c % 2])
        up = conv(d_ff + c * fc, hu_ref.at[c % 2])
        half = 0.5 * gate
        act_ref[:, c * fc:(c + 1) * fc] = (half * (1.0 + jnp.tanh(half)) * up).astype(BF16)
        if (c + 1) % FFN_DOWN_GROUP == 0 or c + 1 == n_chunks:
            lo, hi = group_start * fc, (c + 1) * fc
            part = jnp.dot(act_ref[:, lo:hi], wd_ref[lo:hi, :], preferred_element_type=F32)
            y = part if y is None else y + part
            group_start = c + 1
    o_ref[0] = _layer_norm_rows(DN_ALPHA * x + y, g_ref[...], b_ref[...])


def _ffn(x, w_up, conv_w, conv_b, w_down, ln_g, ln_b):
    B, T, D = x.shape
    d_ff = w_down.shape[0]
    tm, fc = TM_FFN, FC_FFN
    assert d_ff % fc == 0
    halo = _halo_index(tm, HALO)
    const = lambda a: pl.BlockSpec(a.shape, lambda b, i: (0,) * a.ndim, pipeline_mode=pl.Buffered(1))
    conv_b = conv_b.reshape(1, 2 * d_ff)
    ln_g = ln_g.reshape(1, D)
    ln_b = ln_b.reshape(1, D)
    return pl.pallas_call(
        functools.partial(_ffn_kernel, tm=tm, fc=fc, d_ff=d_ff),
        grid=(B, T // tm),
        in_specs=[pl.BlockSpec((1, tm, D), lambda b, i: (b, i, 0)),
                  pl.BlockSpec((1, HALO, D), lambda b, i: (b, halo(i), 0)),
                  const(w_up), const(conv_w), const(conv_b), const(w_down), const(ln_g), const(ln_b)],
        out_specs=pl.BlockSpec((1, tm, D), lambda b, i: (b, i, 0)),
        out_shape=jax.ShapeDtypeStruct((B, T, D), F32),
        scratch_shapes=[pltpu.VMEM((tm + HALO, D), BF16),
                        pltpu.VMEM((2, tm + HALO, fc), F32), pltpu.VMEM((2, tm + HALO, fc), F32),
                        pltpu.VMEM((tm, d_ff), BF16)],
        compiler_params=_cparams(("parallel", "parallel")),
        name="conv_glu_ffn",
    )(x, x, w_up, conv_w, conv_b, w_down, ln_g, ln_b)


def _rope_tables(T):
    half = DIFF_HEAD_DIM // 2
    inv = 1.0 / (ROPE_THETA ** (jnp.arange(0, DIFF_HEAD_DIM, 2, dtype=F32) / DIFF_HEAD_DIM))
    ang = jnp.arange(T, dtype=F32)[:, None] * inv[None, :]
    cos, sin = jnp.cos(ang), jnp.sin(ang)
    reps = LANES // DIFF_HEAD_DIM
    cos_t = jnp.tile(jnp.concatenate([cos, cos], -1), (1, reps))
    sin_t = jnp.tile(jnp.concatenate([-sin, sin], -1), (1, reps))
    assert cos_t.shape == (T, LANES) and half * 2 == DIFF_HEAD_DIM
    return cos_t, sin_t


def _even_layer(x, w_in, conv_w, lq1, lk1, lq2, lk2, subln_g, w_out, lam_init, cos_t, sin_t, ln_g, ln_b):
    B, T, D = x.shape
    W = CONV_WIDTH_MIX
    y_a, q, k, v = _even_in(x, w_in.astype(BF16), conv_w, cos_t, sin_t)
    y_b = _diff_attn(q, k, v, lq1, lk1, lq2, lk2, subln_g, lam_init)
    w_out = w_out.astype(BF16)
    out = _proj_res_ln([y_a.reshape(B * T, W), y_b.reshape(B * T, W)], [w_out[:W], w_out[W:]],
                       x.reshape(B * T, D), ln_g, ln_b)
    return out.reshape(B, T, D)


def _odd_layer(x, mix, w_r, w_k, w_v, w_o, w0, w1, w2, a0, a1, a2, g1, g2, k_k, k_a, r_k, gn_g, gn_b,
               v_first, vres, ln_g, ln_b):
    B, T, D = x.shape
    bf = lambda z: z.astype(BF16)
    if vres is not None:
        vres = (vres[0], bf(vres[1]), bf(vres[2]))
    r, k, v, lw, a, g = _rwkv_proj(x, mix, bf(w_r), bf(w_k), bf(w_v), w0, bf(w1), bf(w2), a0, bf(a1), bf(a2),
                                   bf(g1), bf(g2), vres, v_first)
    if vres is None:
        v_first = v
    o = _wkv(r, k, v, lw, a, g, k_k, k_a, r_k.reshape(-1), gn_g, gn_b)
    out = _proj_res_ln([o.reshape(B * T, D)], [bf(w_o)], x.reshape(B * T, D), ln_g, ln_b)
    return out.reshape(B, T, D), v_first


def kernel(x, ev_w_in, ev_conv_w, ev_lam_q1, ev_lam_k1, ev_lam_q2, ev_lam_k2, ev_subln_g, ev_w_out, rw_mix, rw_w_r, rw_w_k, rw_w_v, rw_w_o, rw_w0, rw_w1, rw_w2, rw_a0, rw_a1, rw_a2, rw_g1, rw_g2, rw_k_k, rw_k_a, rw_r_k, rw_gn_g, rw_gn_b, rw_v0, rw_v1, rw_v2, ffn_w_up, ffn_conv_w, ffn_conv_b, ffn_w_down, ln1_g, ln1_b, ln2_g, ln2_b):
    T = x.shape[1]
    cos_t, sin_t = _rope_tables(T)
    v_first = None
    for l in range(DEPTH):
        if l % 2 == 0:
            i = l // 2
            lam_init = 0.8 - 0.6 * math.exp(-0.3 * l)
            x = _even_layer(x, ev_w_in[i], ev_conv_w[i], ev_lam_q1[i], ev_lam_k1[i], ev_lam_q2[i],
                            ev_lam_k2[i], ev_subln_g[i], ev_w_out[i], lam_init, cos_t, sin_t,
                            ln1_g[l], ln1_b[l])
        else:
            j = l // 2
            vres = None if j == 0 else (rw_v0[j - 1], rw_v1[j - 1], rw_v2[j - 1])
            x, v_first = _odd_layer(x, rw_mix[j], rw_w_r[j], rw_w_k[j], rw_w_v[j], rw_w_o[j], rw_w0[j],
                                    rw_w1[j], rw_w2[j], rw_a0[j], rw_a1[j], rw_a2[j], rw_g1[j], rw_g2[j],
                                    rw_k_k[j], rw_k_a[j], rw_r_k[j], rw_gn_g[j], rw_gn_b[j],
                                    v_first, vres, ln1_g[l], ln1_b[l])
        x = _ffn(x, ffn_w_up[l].astype(BF16), ffn_conv_w[l], ffn_conv_b[l], ffn_w_down[l].astype(BF16),
                 ln2_g[l], ln2_b[l])
    return x
```

```python
import functools
import math

import jax
import jax.numpy as jnp
from jax import lax
from jax.experimental import pallas as pl
from jax.experimental.pallas import tpu as pltpu

F32 = jnp.float32
BF16 = jnp.bfloat16

DEPTH = 4
DN_ALPHA = (2 * DEPTH) ** 0.25
LN_EPS = 1e-5
CONV_WIDTH_MIX = 512
DIFF_HEADS = 4
DIFF_HEAD_DIM = 64
ROPE_THETA = 10000.0
SUBLN_EPS = 1e-5
LOG2_E = math.log2(math.e)
RWKV_HEAD = 64
GN_EPS = 64e-5

LANES = 128
SUBLANES = 8
HALO = 16
VMEM_LIMIT_BYTES = 56 * 1024 * 1024

TM_PROJ = 512
TM_EVEN = 512
TQ_ATT = 256
ATT_HEAD_GROUP = 4
TM_RWKV = 512
WKV_CHUNK = 64
WKV_BATCH = 2
TM_FFN = 512
FC_FFN = 256


def _cparams(semantics):
    return pltpu.CompilerParams(dimension_semantics=semantics, vmem_limit_bytes=VMEM_LIMIT_BYTES)


def _bdot(a, b):
    return jnp.dot(a.astype(BF16), b.astype(BF16), preferred_element_type=F32)


def _bdot_nt(a, b):
    return lax.dot_general(a.astype(BF16), b.astype(BF16), (((1,), (1,)), ((), ())),
                           preferred_element_type=F32)


def _layer_norm_rows(z, g, b):
    mu = jnp.mean(z, -1, keepdims=True)
    zc = z - mu
    var = jnp.mean(zc * zc, -1, keepdims=True)
    return zc * lax.rsqrt(var + LN_EPS) * g + b


def _halo_index(tm, rows):
    per = tm // rows
    return lambda i: jnp.maximum(i * per - 1, 0)


def _proj_res_ln_kernel(*refs, n_pairs):
    a_refs = refs[:n_pairs]
    w_refs = refs[n_pairs:2 * n_pairs]
    x_ref, g_ref, b_ref, o_ref = refs[2 * n_pairs:]
    y = None
    for a_ref, w_ref in zip(a_refs, w_refs):
        d = jnp.dot(a_ref[...], w_ref[...], preferred_element_type=F32)
        y = d if y is None else y + d
    z = DN_ALPHA * x_ref[...] + y
    o_ref[...] = _layer_norm_rows(z, g_ref[...], b_ref[...])


def _proj_res_ln(a_list, w_list, x, g, b):
    M, D = x.shape
    tm = TM_PROJ
    n = len(a_list)
    in_specs = ([pl.BlockSpec((tm, a.shape[1]), lambda i: (i, 0)) for a in a_list]
                + [pl.BlockSpec(w.shape, lambda i: (0, 0)) for w in w_list]
                + [pl.BlockSpec((tm, D), lambda i: (i, 0)),
                   pl.BlockSpec((1, D), lambda i: (0, 0)),
                   pl.BlockSpec((1, D), lambda i: (0, 0))])
    return pl.pallas_call(
        functools.partial(_proj_res_ln_kernel, n_pairs=n),
        grid=(M // tm,),
        in_specs=in_specs,
        out_specs=pl.BlockSpec((tm, D), lambda i: (i, 0)),
        out_shape=jax.ShapeDtypeStruct((M, D), F32),
        compiler_params=_cparams(("parallel",)),
        name="proj_res_ln",
    )(*a_list, *w_list, x, g.reshape(1, D), b.reshape(1, D))


def _rope_block(z, c, s, first_half):
    half = DIFF_HEAD_DIM // 2
    partner = jnp.where(first_half, pltpu.roll(z, LANES - half, 1), pltpu.roll(z, half, 1))
    return z * c + partner * s


def _even_in_kernel(x_ref, xh_ref, w_ref, cw_ref, cos_ref, sin_ref,
                    ya_ref, q_ref, k_ref, v_ref, xcat_ref, ubuf, *, tm):
    i = pl.program_id(1)
    W = CONV_WIDTH_MIX
    xcat_ref[0:HALO, :] = (xh_ref[0] * (i > 0).astype(F32)).astype(BF16)
    xcat_ref[HALO:, :] = x_ref[0].astype(BF16)
    xb = xcat_ref[HALO:, :]
    gcx = jnp.dot(xcat_ref[...], w_ref[:, W:3 * W], preferred_element_type=F32)
    ubuf[...] = gcx[:, :W] * gcx[:, W:]
    gb = jnp.dot(xb, w_ref[:, 0:W], preferred_element_type=F32)
    cw = cw_ref[...]
    conv = (cw[2:3] * ubuf[pl.ds(HALO, tm), :] + cw[1:2] * ubuf[pl.ds(HALO - 1, tm), :]
            + cw[0:1] * ubuf[pl.ds(HALO - 2, tm), :])
    ya_ref[0] = (gb * conv).astype(BF16)

    lane = lax.broadcasted_iota(jnp.int32, (1, LANES), 1)
    first_half = (lane % DIFF_HEAD_DIM) < (DIFF_HEAD_DIM // 2)
    c = cos_ref[...]
    s = sin_ref[...]
    scale = DIFF_HEAD_DIM ** -0.5 * LOG2_E
    q = jnp.dot(xb, w_ref[:, 3 * W:4 * W], preferred_element_type=F32)
    k = jnp.dot(xb, w_ref[:, 4 * W:5 * W], preferred_element_type=F32)
    for blk in range(W // LANES):
        sl = slice(blk * LANES, (blk + 1) * LANES)
        q_ref[0, :, sl] = (_rope_block(q[:, sl], c, s, first_half) * scale).astype(BF16)
        k_ref[0, :, sl] = _rope_block(k[:, sl], c, s, first_half).astype(BF16)
    v_ref[0] = jnp.dot(xb, w_ref[:, 5 * W:6 * W], preferred_element_type=F32).astype(BF16)


def _even_in(x, w_in, conv_w, cos_t, sin_t):
    B, T, D = x.shape
    tm = TM_EVEN
    W = CONV_WIDTH_MIX
    halo = _halo_index(tm, HALO)
    out = jax.ShapeDtypeStruct((B, T, W), BF16)
    ospec = pl.BlockSpec((1, tm, W), lambda b, i: (b, i, 0))
    return pl.pallas_call(
        functools.partial(_even_in_kernel, tm=tm),
        grid=(B, T // tm),
        in_specs=[pl.BlockSpec((1, tm, D), lambda b, i: (b, i, 0)),
                  pl.BlockSpec((1, HALO, D), lambda b, i: (b, halo(i), 0)),
                  pl.BlockSpec(w_in.shape, lambda b, i: (0, 0), pipeline_mode=pl.Buffered(1)),
                  pl.BlockSpec(conv_w.shape, lambda b, i: (0, 0)),
                  pl.BlockSpec((tm, LANES), lambda b, i: (i, 0)),
                  pl.BlockSpec((tm, LANES), lambda b, i: (i, 0))],
        out_specs=[ospec, ospec, ospec, ospec],
        out_shape=[out, out, out, out],
        scratch_shapes=[pltpu.VMEM((tm + HALO, D), BF16), pltpu.VMEM((tm + HALO, W), F32)],
        compiler_params=_cparams(("parallel", "parallel")),
        name="even_in_proj",
    )(x, x, w_in, conv_w, cos_t, sin_t)


def _diff_attn_kernel(q_ref, k_ref, v_ref, lq1_ref, lk1_ref, lq2_ref, lk2_ref, g_ref, o_ref,
                      qs_ref, m_ref, l_ref, acc_ref, *, tq, lam_init):
    qi = pl.program_id(1)
    heads = range(DIFF_HEADS)
    lanes_of = lambda h: slice(h * LANES, (h + 1) * LANES)
    lam = (jnp.exp(jnp.sum(lq1_ref[...] * lk1_ref[...], keepdims=True))
           - jnp.exp(jnp.sum(lq2_ref[...] * lk2_ref[...], keepdims=True)) + lam_init)
    lane = lax.broadcasted_iota(jnp.int32, (1, LANES), 1)
    map0 = lane < DIFF_HEAD_DIM
    for h in heads:
        q = q_ref[0, :, lanes_of(h)]
        zero = jnp.zeros_like(q)
        qs_ref[h] = jnp.concatenate([jnp.where(map0, q, zero), jnp.where(map0, zero, q)], axis=0)
    m_ref[...] = jnp.full(m_ref.shape, -1e30, F32)
    l_ref[...] = jnp.zeros(l_ref.shape, F32)
    acc_ref[...] = jnp.zeros(acc_ref.shape, F32)

    def chunk(start, tk, diag_offset):
        for first in range(0, DIFF_HEADS, ATT_HEAD_GROUP):
            chunk_heads(start, tk, diag_offset, range(first, first + ATT_HEAD_GROUP))

    def chunk_heads(start, tk, diag_offset, group):
        st = {h: lax.dot_general(k_ref[0, pl.ds(start, tk), lanes_of(h)], qs_ref[h], (((1,), (1,)), ((), ())),
                                 preferred_element_type=F32) for h in group}
        if diag_offset is not None:
            key = lax.broadcasted_iota(jnp.int32, (tk, 2 * tq), 0)
            qry = lax.broadcasted_iota(jnp.int32, (tk, 2 * tq), 1) % tq + diag_offset
            st = {h: jnp.where(key <= qry, s, -1e30) for h, s in st.items()}
        m_old = {h: m_ref[h] for h in group}
        m_new = {h: jnp.maximum(m_old[h], jnp.max(st[h], 0, keepdims=True)) for h in group}
        pt = {h: jnp.exp2(st[h] - m_new[h]) for h in group}
        corr = {h: jnp.exp2(m_old[h] - m_new[h]) for h in group}
        pv = {h: lax.dot_general(v_ref[0, pl.ds(start, tk), lanes_of(h)], pt[h].astype(BF16),
                                 (((0,), (0,)), ((), ())), preferred_element_type=F32)
              for h in group}
        for h in group:
            m_ref[h] = m_new[h]
            l_ref[h] = l_ref[h] * corr[h] + jnp.sum(pt[h], 0, keepdims=True)
            acc_ref[h] = acc_ref[h] * corr[h] + pv[h]

    @pl.loop(0, qi // 2)
    def _(j):
        chunk(pl.multiple_of(j * (2 * tq), 2 * tq), 2 * tq, None)

    @pl.when(qi % 2 == 1)
    def _():
        chunk(pl.multiple_of((qi - 1) * tq, tq), 2 * tq, tq)

    @pl.when(qi % 2 == 0)
    def _():
        chunk(pl.multiple_of(qi * tq, tq), tq, 0)

    for h in heads:
        acc = acc_ref[h]
        inv_l = 1.0 / l_ref[h]
        o = acc[:, :tq] * inv_l[:, :tq] - (lam * inv_l[:, tq:]) * acc[:, tq:]
        ms = jnp.mean(o * o, 0, keepdims=True)
        o = o * lax.rsqrt(ms + SUBLN_EPS) * g_ref[...]
        o_ref[0, :, lanes_of(h)] = (o * (1.0 - lam_init)).T.astype(BF16)


def _diff_attn(q, k, v, lq1, lk1, lq2, lk2, subln_g, lam_init):
    B, T, W = q.shape
    tq = TQ_ATT
    vec = lambda a: a.reshape(1, -1)
    vspec = lambda n: pl.BlockSpec((1, n), lambda b, i: (0, 0))
    return pl.pallas_call(
        functools.partial(_diff_attn_kernel, tq=tq, lam_init=lam_init),
        grid=(B, T // tq),
        in_specs=[pl.BlockSpec((1, tq, W), lambda b, i: (b, i, 0)),
                  pl.BlockSpec((1, T, W), lambda b, i: (b, 0, 0)),
                  pl.BlockSpec((1, T, W), lambda b, i: (b, 0, 0)),
                  vspec(DIFF_HEAD_DIM), vspec(DIFF_HEAD_DIM), vspec(DIFF_HEAD_DIM), vspec(DIFF_HEAD_DIM),
                  pl.BlockSpec((2 * DIFF_HEAD_DIM, 1), lambda b, i: (0, 0))],
        out_specs=pl.BlockSpec((1, tq, W), lambda b, i: (b, i, 0)),
        out_shape=jax.ShapeDtypeStruct((B, T, W), BF16),
        scratch_shapes=[pltpu.VMEM((DIFF_HEADS, 2 * tq, LANES), BF16),
                        pltpu.VMEM((DIFF_HEADS, 1, 2 * tq), F32),
                        pltpu.VMEM((DIFF_HEADS, 1, 2 * tq), F32),
                        pltpu.VMEM((DIFF_HEADS, LANES, 2 * tq), F32)],
        compiler_params=_cparams(("parallel", "parallel")),
        name="diff_attention",
    )(q, k, v, vec(lq1), vec(lk1), vec(lq2), vec(lk2), subln_g.reshape(-1, 1))


def _sigmoid(z):
    return 0.5 + 0.5 * jnp.tanh(0.5 * z)


def _rwkv_proj_kernel(*refs, tm, has_vres):
    if has_vres:
        (x_ref, xh_ref, mix_ref, wr_ref, wk_ref, wv_ref, w0_ref, w1_ref, w2_ref, a0_ref, a1_ref, a2_ref,
         g1_ref, g2_ref, v0_ref, v1_ref, v2_ref, vf_ref,
         r_ref, k_ref, v_ref, lw_ref, a_ref, g_ref, xbuf) = refs
    else:
        (x_ref, xh_ref, mix_ref, wr_ref, wk_ref, wv_ref, w0_ref, w1_ref, w2_ref, a0_ref, a1_ref, a2_ref,
         g1_ref, g2_ref,
         r_ref, k_ref, v_ref, lw_ref, a_ref, g_ref, xbuf) = refs
    i = pl.program_id(1)
    x = x_ref[0]
    xbuf[0:SUBLANES, :] = xh_ref[0] * (i > 0).astype(F32)
    xbuf[SUBLANES:, :] = x
    xx = xbuf[pl.ds(SUBLANES - 1, tm), :] - x
    mix = mix_ref[...]

    def mixed(n):
        return (x + xx * mix[n:n + 1]).astype(BF16)

    r_ref[0] = jnp.dot(mixed(0), wr_ref[...], preferred_element_type=F32)
    xw = mixed(1)
    wpre = w0_ref[...] + _bdot(jnp.tanh(jnp.dot(xw, w1_ref[...], preferred_element_type=F32)), w2_ref[...])
    w_log = jnp.minimum(wpre, 0.0) - jnp.log(1.0 + jnp.exp(-jnp.abs(wpre))) - 0.5
    lw_ref[0] = -jnp.exp(w_log)
    k_ref[0] = jnp.dot(mixed(2), wk_ref[...], preferred_element_type=F32)
    xv = mixed(3)
    v = jnp.dot(xv, wv_ref[...], preferred_element_type=F32)
    if has_vres:
        gate = _sigmoid(v0_ref[...] + _bdot(jnp.dot(xv, v1_ref[...], preferred_element_type=F32), v2_ref[...]))
        v = v + (vf_ref[0] - v) * gate
    v_ref[0] = v
    xa = mixed(4)
    a_ref[0] = _sigmoid(a0_ref[...] + _bdot(jnp.dot(xa, a1_ref[...], preferred_element_type=F32), a2_ref[...]))
    xg = mixed(5)
    g_ref[0] = _bdot(_sigmoid(jnp.dot(xg, g1_ref[...], preferred_element_type=F32)), g2_ref[...]).astype(BF16)


def _rwkv_proj(x, mix, w_r, w_k, w_v, w0, w1, w2, a0, a1, a2, g1, g2, vres, v_first):
    B, T, D = x.shape
    tm = TM_RWKV
    halo = _halo_index(tm, SUBLANES)
    has_vres = vres is not None
    full = lambda a: pl.BlockSpec(a.shape, lambda b, i: (0,) * a.ndim, pipeline_mode=pl.Buffered(1))
    tile = pl.BlockSpec((1, tm, D), lambda b, i: (b, i, 0))
    row = lambda a: a.reshape(1, D)
    args = [x, x, mix, w_r, w_k, w_v, row(w0), w1, w2, row(a0), a1, a2, g1, g2]
    if has_vres:
        v0, v1, v2 = vres
        args += [row(v0), v1, v2, v_first]
    in_specs = ([tile, pl.BlockSpec((1, SUBLANES, D), lambda b, i: (b, halo(i), 0))]
                + [full(a) for a in args[2:14]])
    if has_vres:
        in_specs += [full(args[14]), full(args[15]), full(args[16]), tile]
    f32o = jax.ShapeDtypeStruct((B, T, D), F32)
    return pl.pallas_call(
        functools.partial(_rwkv_proj_kernel, tm=tm, has_vres=has_vres),
        grid=(B, T // tm),
        in_specs=in_specs,
        out_specs=[tile] * 6,
        out_shape=[f32o, f32o, f32o, f32o, f32o, jax.ShapeDtypeStruct((B, T, D), BF16)],
        scratch_shapes=[pltpu.VMEM((tm + SUBLANES, D), F32)],
        compiler_params=_cparams(("parallel", "parallel")),
        name="rwkv_proj",
    )(*args)


def _stack_heads(z, head0):
    zero = jnp.zeros_like(z)
    return jnp.concatenate([jnp.where(head0, z, zero), jnp.where(head0, zero, z)], axis=0)


def _head_sum(z, head0):
    zero = jnp.zeros_like(z)
    s0 = jnp.sum(jnp.where(head0, z, zero), -1, keepdims=True)
    s1 = jnp.sum(jnp.where(head0, zero, z), -1, keepdims=True)
    return jnp.where(head0, s0, s1)


def _unit_lower_inverse(Ls, row, col, n_time):
    zero = jnp.zeros_like(Ls[0])
    eye = (row == col).astype(F32)
    base = (row // 2) == (col // 2)
    ts = [eye + jnp.where(base, L, zero) for L in Ls]
    s = 2
    while s < n_time:
        lower_left = ((row // (2 * s)) == (col // (2 * s))) & ((row // s) % 2 == 1) & ((col // s) % 2 == 0)
        offs = [_bdot(t, jnp.where(lower_left, L, zero)) for t, L in zip(ts, Ls)]
        ts = [t + _bdot(o, t) for t, o in zip(ts, offs)]
        s *= 2
    return ts


def _wkv_kernel(r_ref, k_ref, v_ref, lw_ref, a_ref, g_ref, kk_ref, ka_ref, rk_ref, gng_ref, gnb_ref,
                o_ref, st_ref, *, C, n_pairs):
    @pl.when(pl.program_id(1) == 0)
    def _():
        st_ref[...] = jnp.zeros_like(st_ref)

    P = 2 * C
    lane = lax.broadcasted_iota(jnp.int32, (1, LANES), 1)
    head0 = lane < RWKV_HEAD
    row = lax.broadcasted_iota(jnp.int32, (P, P), 0)
    col = lax.broadcasted_iota(jnp.int32, (P, P), 1)
    same_head = (row // C) == (col // C)
    strict = same_head & (col < row)
    incl = same_head & (col <= row)
    trow = lax.broadcasted_iota(jnp.int32, (C, LANES), 0)
    zero_pp = jnp.zeros((P, P), F32)
    pairs = range(o_ref.shape[0] * n_pairs)
    lanes_of = lambda p: slice((p % n_pairs) * LANES, (p % n_pairs + 1) * LANES)

    def prepare(p):
        sl = lanes_of(p)
        r = r_ref[p // n_pairs, :, sl]
        k = k_ref[p // n_pairs, :, sl]
        lw = lw_ref[p // n_pairs, :, sl]
        a = a_ref[p // n_pairs, :, sl]
        kk = k * kk_ref[:, sl]
        norm = jnp.sqrt(_head_sum(kk * kk, head0))
        kk = kk / jnp.maximum(norm, 1e-12)
        k2 = k * (1.0 + (a - 1.0) * ka_ref[:, sl])
        cum = lw
        sh = 1
        while sh < C:
            cum = cum + jnp.where(trow >= sh, pltpu.roll(cum, sh, 0), 0.0)
            sh *= 2
        cum_end = cum[C - 1:C, :]
        e_pos = jnp.exp(cum)
        e_neg = jnp.exp(-cum)
        e_tail = jnp.exp(cum_end - cum)
        b_t = kk * a
        return dict(
            a_s=_stack_heads(-kk * jnp.exp(cum - lw), head0),
            b_s=_stack_heads(b_t * e_neg, head0),
            k_s=_stack_heads(k2 * e_neg, head0),
            r_s=_stack_heads(r * e_pos, head0),
            bh_s=_stack_heads(b_t * e_tail, head0),
            kh_s=_stack_heads(k2 * e_tail, head0),
            v_s=_stack_heads(v_ref[p // n_pairs, :, sl], head0),
            decay_end=jnp.exp(cum_end))

    d, grams = [], []
    for p in pairs:
        d.append(prepare(p))
        grams.append(_bdot_nt(jnp.concatenate([d[p]["a_s"], d[p]["r_s"]], 0),
                              jnp.concatenate([d[p]["b_s"], d[p]["k_s"]], 0)))
    l_ab = [jnp.where(strict, g[:P, :P], zero_pp) for g in grams]
    lv = [_bdot(jnp.where(strict, grams[p][:P, P:], zero_pp), d[p]["v_s"]) for p in pairs]
    m_r = [jnp.concatenate([jnp.where(incl, g[P:, :P], zero_pp), jnp.where(incl, g[P:, P:], zero_pp)], 1)
           for g in grams]
    t_inv = _unit_lower_inverse(l_ab, row, col, C)
    ty = [_bdot(t_inv[p], jnp.concatenate([d[p]["a_s"], lv[p]], 1)) for p in pairs]
    sts = [st_ref[p] for p in pairs]
    zs = [_bdot_nt(jnp.concatenate([ty[p][:, :LANES], d[p]["r_s"]], 0), sts[p]) for p in pairs]
    uvs = [jnp.concatenate([zs[p][:P] + ty[p][:, LANES:], d[p]["v_s"]], 0) for p in pairs]
    o_s = [zs[p][P:] + _bdot(m_r[p], uvs[p]) for p in pairs]
    for p in pairs:
        st_ref[p] = sts[p] * d[p]["decay_end"] + _bdot(uvs[p].T, jnp.concatenate([d[p]["bh_s"], d[p]["kh_s"]], 0))
        sl = lanes_of(p)
        o = o_s[p][:C] + o_s[p][C:]
        r = r_ref[p // n_pairs, :, sl]
        k2 = k_ref[p // n_pairs, :, sl] * (1.0 + (a_ref[p // n_pairs, :, sl] - 1.0) * ka_ref[:, sl])
        mu = _head_sum(o, head0) * (1.0 / RWKV_HEAD)
        oc = o - mu
        var = _head_sum(oc * oc, head0) * (1.0 / RWKV_HEAD)
        on = oc * lax.rsqrt(var + GN_EPS) * gng_ref[:, sl] + gnb_ref[:, sl]
        bonus = _head_sum(r * k2 * rk_ref[:, sl], head0) * v_ref[p // n_pairs, :, sl]
        o_ref[p // n_pairs, :, sl] = ((on + bonus) * g_ref[p // n_pairs, :, sl].astype(F32)).astype(BF16)


def _wkv(r, k, v, lw, a, g, k_k, k_a, r_k, gn_g, gn_b):
    B, T, D = r.shape
    C = WKV_CHUNK
    n_pairs = D // LANES
    nb = WKV_BATCH
    tile = pl.BlockSpec((nb, C, D), lambda b, c: (b, c, 0))
    vspec = pl.BlockSpec((1, D), lambda b, c: (0, 0))
    row = lambda z: z.reshape(1, D)
    return pl.pallas_call(
        functools.partial(_wkv_kernel, C=C, n_pairs=n_pairs),
        grid=(B // nb, T // C),
        in_specs=[tile] * 6 + [vspec] * 5,
        out_specs=tile,
        out_shape=jax.ShapeDtypeStruct((B, T, D), BF16),
        scratch_shapes=[pltpu.VMEM((nb * n_pairs, LANES, LANES), F32)],
        compiler_params=_cparams(("parallel", "arbitrary")),
        name="wkv7",
    )(r, k, v, lw, a, g, row(k_k), row(k_a), row(r_k), row(gn_g), row(gn_b))


def _ffn_kernel(x_ref, xh_ref, wup_ref, cw_ref, cb_ref, wd_ref, g_ref, b_ref,
                o_ref, xcat_ref, hg_ref, hu_ref, act_ref, *, tm, fc, d_ff):
    i = pl.program_id(1)
    x = x_ref[0]
    xcat_ref[0:HALO, :] = (xh_ref[0] * (i > 0).astype(F32)).astype(BF16)
    xcat_ref[HALO:, :] = x.astype(BF16)

    n_chunks = d_ff // fc

    def up_project(c):
        for col, h_ref in ((c * fc, hg_ref), (d_ff + c * fc, hu_ref)):
            h_ref[c % 2] = jnp.dot(xcat_ref[...], wup_ref[:, col:col + fc], preferred_element_type=F32)

    def conv(col, h_ref):
        cw = cw_ref[:, col:col + fc]
        return (cw[2:3] * h_ref[pl.ds(HALO, tm), :] + cw[1:2] * h_ref[pl.ds(HALO - 1, tm), :]
                + cw[0:1] * h_ref[pl.ds(HALO - 2, tm), :] + cb_ref[:, col:col + fc])

    up_project(0)
    for c in range(n_chunks):
        if c + 1 < n_chunks:
            up_project(c + 1)
        gate = conv(c * fc, hg_ref.at[c % 2])
        up = conv(d_ff + c * fc, hu_ref.at[c % 2])
        half = 0.5 * gate
        act_ref[:, c * fc:(c + 1) * fc] = (half * (1.0 + jnp.tanh(half)) * up).astype(BF16)
    y = jnp.dot(act_ref[...], wd_ref[...], preferred_element_type=F32)
    o_ref[0] = _layer_norm_rows(DN_ALPHA * x + y, g_ref[...], b_ref[...])


def _ffn(x, w_up, conv_w, conv_b, w_down, ln_g, ln_b):
    B, T, D = x.shape
    d_ff = w_down.shape[0]
    tm, fc = TM_FFN, FC_FFN
    assert d_ff % fc == 0
    halo = _halo_index(tm, HALO)
    const = lambda a: pl.BlockSpec(a.shape, lambda b, i: (0,) * a.ndim, pipeline_mode=pl.Buffered(1))
    conv_b = conv_b.reshape(1, 2 * d_ff)
    ln_g = ln_g.reshape(1, D)
    ln_b = ln_b.reshape(1, D)
    return pl.pallas_call(
        functools.partial(_ffn_kernel, tm=tm, fc=fc, d_ff=d_ff),
        grid=(B, T // tm),
        in_specs=[pl.BlockSpec((1, tm, D), lambda b, i: (b, i, 0)),
                  pl.BlockSpec((1, HALO, D), lambda b, i: (b, halo(i), 0)),
                  const(w_up), const(conv_w), const(conv_b), const(w_down), const(ln_g), const(ln_b)],
        out_specs=pl.BlockSpec((1, tm, D), lambda b, i: (b, i, 0)),
        out_shape=jax.ShapeDtypeStruct((B, T, D), F32),
        scratch_shapes=[pltpu.VMEM((tm + HALO, D), BF16),
                        pltpu.VMEM((2, tm + HALO, fc), F32), pltpu.VMEM((2, tm + HALO, fc), F32),
                        pltpu.VMEM((tm, d_ff), BF16)],
        compiler_params=_cparams(("parallel", "parallel")),
        name="conv_glu_ffn",
    )(x, x, w_up, conv_w, conv_b, w_down, ln_g, ln_b)


def _rope_tables(T):
    half = DIFF_HEAD_DIM // 2
    inv = 1.0 / (ROPE_THETA ** (jnp.arange(0, DIFF_HEAD_DIM, 2, dtype=F32) / DIFF_HEAD_DIM))
    ang = jnp.arange(T, dtype=F32)[:, None] * inv[None, :]
    cos, sin = jnp.cos(ang), jnp.sin(ang)
    reps = LANES // DIFF_HEAD_DIM
    cos_t = jnp.tile(jnp.concatenate([cos, cos], -1), (1, reps))
    sin_t = jnp.tile(jnp.concatenate([-sin, sin], -1), (1, reps))
    assert cos_t.shape == (T, LANES) and half * 2 == DIFF_HEAD_DIM
    return cos_t, sin_t


def _even_layer(x, w_in, conv_w, lq1, lk1, lq2, lk2, subln_g, w_out, lam_init, cos_t, sin_t, ln_g, ln_b):
    B, T, D = x.shape
    W = CONV_WIDTH_MIX
    y_a, q, k, v = _even_in(x, w_in.astype(BF16), conv_w, cos_t, sin_t)
    y_b = _diff_attn(q, k, v, lq1, lk1, lq2, lk2, subln_g, lam_init)
    w_out = w_out.astype(BF16)
    out = _proj_res_ln([y_a.reshape(B * T, W), y_b.reshape(B * T, W)], [w_out[:W], w_out[W:]],
                       x.reshape(B * T, D), ln_g, ln_b)
    return out.reshape(B, T, D)


def _odd_layer(x, mix, w_r, w_k, w_v, w_o, w0, w1, w2, a0, a1, a2, g1, g2, k_k, k_a, r_k, gn_g, gn_b,
               v_first, vres, ln_g, ln_b):
    B, T, D = x.shape
    bf = lambda z: z.astype(BF16)
    if vres is not None:
        vres = (vres[0], bf(vres[1]), bf(vres[2]))
    r, k, v, lw, a, g = _rwkv_proj(x, mix, bf(w_r), bf(w_k), bf(w_v), w0, bf(w1), bf(w2), a0, bf(a1), bf(a2),
                                   bf(g1), bf(g2), vres, v_first)
    if vres is None:
        v_first = v
    o = _wkv(r, k, v, lw, a, g, k_k, k_a, r_k.reshape(-1), gn_g, gn_b)
    out = _proj_res_ln([o.reshape(B * T, D)], [bf(w_o)], x.reshape(B * T, D), ln_g, ln_b)
    return out.reshape(B, T, D), v_first


def kernel(x, ev_w_in, ev_conv_w, ev_lam_q1, ev_lam_k1, ev_lam_q2, ev_lam_k2, ev_subln_g, ev_w_out, rw_mix, rw_w_r, rw_w_k, rw_w_v, rw_w_o, rw_w0, rw_w1, rw_w2, rw_a0, rw_a1, rw_a2, rw_g1, rw_g2, rw_k_k, rw_k_a, rw_r_k, rw_gn_g, rw_gn_b, rw_v0, rw_v1, rw_v2, ffn_w_up, ffn_conv_w, ffn_conv_b, ffn_w_down, ln1_g, ln1_b, ln2_g, ln2_b):
    T = x.shape[1]
    cos_t, sin_t = _rope_tables(T)
    v_first = None
    for l in range(DEPTH):
        if l % 2 == 0:
            i = l // 2
            lam_init = 0.8 - 0.6 * math.exp(-0.3 * l)
            x = _even_layer(x, ev_w_in[i], ev_conv_w[i], ev_lam_q1[i], ev_lam_k1[i], ev_lam_q2[i],
                            ev_lam_k2[i], ev_subln_g[i], ev_w_out[i], lam_init, cos_t, sin_t,
                            ln1_g[l], ln1_b[l])
        else:
            j = l // 2
            vres = None if j == 0 else (rw_v0[j - 1], rw_v1[j - 1], rw_v2[j - 1])
            x, v_first = _odd_layer(x, rw_mix[j], rw_w_r[j], rw_w_k[j], rw_w_v[j], rw_w_o[j], rw_w0[j],
                                    rw_w1[j], rw_w2[j], rw_a0[j], rw_a1[j], rw_a2[j], rw_g1[j], rw_g2[j],
                                    rw_k_k[j], rw_k_a[j], rw_r_k[j], rw_gn_g[j], rw_gn_b[j],
                                    v_first, vres, ln1_g[l], ln1_b[l])
        x = _ffn(x, ffn_w_up[l].astype(BF16), ffn_conv_w[l], ffn_conv_b[l], ffn_w_down[l].astype(BF16),
                 ln2_g[l], ln2_b[l])
    return x
```

```python
import functools
import math

import jax
import jax.numpy as jnp
from jax import lax
from jax.experimental import pallas as pl
from jax.experimental.pallas import tpu as pltpu

F32 = jnp.float32
BF16 = jnp.bfloat16

DEPTH = 4
DN_ALPHA = (2 * DEPTH) ** 0.25
LN_EPS = 1e-5
CONV_WIDTH_MIX = 512
DIFF_HEADS = 4
DIFF_HEAD_DIM = 64
ROPE_THETA = 10000.0
SUBLN_EPS = 1e-5
LOG2_E = math.log2(math.e)
RWKV_HEAD = 64
GN_EPS = 64e-5

LANES = 128
SUBLANES = 8
HALO = 16
VMEM_LIMIT_BYTES = 56 * 1024 * 1024

TM_PROJ = 512
TM_EVEN = 512
TQ_ATT = 256
ATT_HEAD_GROUP = 4
TM_RWKV = 512
WKV_CHUNK = 64
WKV_BATCH = 2
TM_FFN = 512
FC_FFN = 256


def _cparams(semantics):
    return pltpu.CompilerParams(dimension_semantics=semantics, vmem_limit_bytes=VMEM_LIMIT_BYTES)


def _bdot(a, b):
    return jnp.dot(a.astype(BF16), b.astype(BF16), preferred_element_type=F32)


def _bdot_nt(a, b):
    return lax.dot_general(a.astype(BF16), b.astype(BF16), (((1,), (1,)), ((), ())),
                           preferred_element_type=F32)


def _layer_norm_rows(z, g, b):
    mu = jnp.mean(z, -1, keepdims=True)
    zc = z - mu
    var = jnp.mean(zc * zc, -1, keepdims=True)
    return zc * lax.rsqrt(var + LN_EPS) * g + b


def _halo_index(tm, rows):
    per = tm // rows
    return lambda i: jnp.maximum(i * per - 1, 0)


def _proj_res_ln_kernel(*refs, n_pairs):
    a_refs = refs[:n_pairs]
    w_refs = refs[n_pairs:2 * n_pairs]
    x_ref, g_ref, b_ref, o_ref = refs[2 * n_pairs:]
    y = None
    for a_ref, w_ref in zip(a_refs, w_refs):
        d = jnp.dot(a_ref[...], w_ref[...], preferred_element_type=F32)
        y = d if y is None else y + d
    z = DN_ALPHA * x_ref[...] + y
    o_ref[...] = _layer_norm_rows(z, g_ref[...], b_ref[...])


def _proj_res_ln(a_list, w_list, x, g, b):
    M, D = x.shape
    tm = TM_PROJ
    n = len(a_list)
    in_specs = ([pl.BlockSpec((tm, a.shape[1]), lambda i: (i, 0)) for a in a_list]
                + [pl.BlockSpec(w.shape, lambda i: (0, 0)) for w in w_list]
                + [pl.BlockSpec((tm, D), lambda i: (i, 0)),
                   pl.BlockSpec((1, D), lambda i: (0, 0)),
                   pl.BlockSpec((1, D), lambda i: (0, 0))])
    return pl.pallas_call(
        functools.partial(_proj_res_ln_kernel, n_pairs=n),
        grid=(M // tm,),
        in_specs=in_specs,
        out_specs=pl.BlockSpec((tm, D), lambda i: (i, 0)),
        out_shape=jax.ShapeDtypeStruct((M, D), F32),
        compiler_params=_cparams(("parallel",)),
        name="proj_res_ln",
    )(*a_list, *w_list, x, g.reshape(1, D), b.reshape(1, D))


def _rope_block(z, c, s, first_half):
    half = DIFF_HEAD_DIM // 2
    partner = jnp.where(first_half, pltpu.roll(z, LANES - half, 1), pltpu.roll(z, half, 1))
    return z * c + partner * s


def _even_in_kernel(x_ref, xh_ref, w_ref, cw_ref, cos_ref, sin_ref,
                    ya_ref, q_ref, k_ref, v_ref, xcat_ref, ubuf, *, tm):
    i = pl.program_id(1)
    W = CONV_WIDTH_MIX
    xcat_ref[0:HALO, :] = (xh_ref[0] * (i > 0).astype(F32)).astype(BF16)
    xcat_ref[HALO:, :] = x_ref[0].astype(BF16)
    xb = xcat_ref[HALO:, :]
    gcx = jnp.dot(xcat_ref[...], w_ref[:, W:3 * W], preferred_element_type=F32)
    ubuf[...] = gcx[:, :W] * gcx[:, W:]
    gb = jnp.dot(xb, w_ref[:, 0:W], preferred_element_type=F32)
    cw = cw_ref[...]
    conv = (cw[2:3] * ubuf[pl.ds(HALO, tm), :] + cw[1:2] * ubuf[pl.ds(HALO - 1, tm), :]
            + cw[0:1] * ubuf[pl.ds(HALO - 2, tm), :])
    ya_ref[0] = (gb * conv).astype(BF16)

    lane = lax.broadcasted_iota(jnp.int32, (1, LANES), 1)
    first_half = (lane % DIFF_HEAD_DIM) < (DIFF_HEAD_DIM // 2)
    c = cos_ref[...]
    s = sin_ref[...]
    scale = DIFF_HEAD_DIM ** -0.5 * LOG2_E
    q = jnp.dot(xb, w_ref[:, 3 * W:4 * W], preferred_element_type=F32)
    k = jnp.dot(xb, w_ref[:, 4 * W:5 * W], preferred_element_type=F32)
    for blk in range(W // LANES):
        sl = slice(blk * LANES, (blk + 1) * LANES)
        q_ref[0, :, sl] = (_rope_block(q[:, sl], c, s, first_half) * scale).astype(BF16)
        k_ref[0, :, sl] = _rope_block(k[:, sl], c, s, first_half).astype(BF16)
    v_ref[0] = jnp.dot(xb, w_ref[:, 5 * W:6 * W], preferred_element_type=F32).astype(BF16)


def _even_in(x, w_in, conv_w, cos_t, sin_t):
    B, T, D = x.shape
    tm = TM_EVEN
    W = CONV_WIDTH_MIX
    halo = _halo_index(tm, HALO)
    out = jax.ShapeDtypeStruct((B, T, W), BF16)
    ospec = pl.BlockSpec((1, tm, W), lambda b, i: (b, i, 0))
    return pl.pallas_call(
        functools.partial(_even_in_kernel, tm=tm),
        grid=(B, T // tm),
        in_specs=[pl.BlockSpec((1, tm, D), lambda b, i: (b, i, 0)),
                  pl.BlockSpec((1, HALO, D), lambda b, i: (b, halo(i), 0)),
                  pl.BlockSpec(w_in.shape, lambda b, i: (0, 0), pipeline_mode=pl.Buffered(1)),
                  pl.BlockSpec(conv_w.shape, lambda b, i: (0, 0)),
                  pl.BlockSpec((tm, LANES), lambda b, i: (i, 0)),
                  pl.BlockSpec((tm, LANES), lambda b, i: (i, 0))],
        out_specs=[ospec, ospec, ospec, ospec],
        out_shape=[out, out, out, out],
        scratch_shapes=[pltpu.VMEM((tm + HALO, D), BF16), pltpu.VMEM((tm + HALO, W), F32)],
        compiler_params=_cparams(("parallel", "parallel")),
        name="even_in_proj",
    )(x, x, w_in, conv_w, cos_t, sin_t)


def _diff_attn_kernel(q_ref, k_ref, v_ref, lq1_ref, lk1_ref, lq2_ref, lk2_ref, g_ref, o_ref,
                      qs_ref, m_ref, l_ref, acc_ref, *, tq, lam_init):
    qi = pl.program_id(1)
    heads = range(DIFF_HEADS)
    lanes_of = lambda h: slice(h * LANES, (h + 1) * LANES)
    lam = (jnp.exp(jnp.sum(lq1_ref[...] * lk1_ref[...], keepdims=True))
           - jnp.exp(jnp.sum(lq2_ref[...] * lk2_ref[...], keepdims=True)) + lam_init)
    lane = lax.broadcasted_iota(jnp.int32, (1, LANES), 1)
    map0 = lane < DIFF_HEAD_DIM
    for h in heads:
        q = q_ref[0, :, lanes_of(h)]
        zero = jnp.zeros_like(q)
        qs_ref[h] = jnp.concatenate([jnp.where(map0, q, zero), jnp.where(map0, zero, q)], axis=0)
    m_ref[...] = jnp.full(m_ref.shape, -1e30, F32)
    l_ref[...] = jnp.zeros(l_ref.shape, F32)
    acc_ref[...] = jnp.zeros(acc_ref.shape, F32)

    def chunk(start, tk, diag_offset):
        for first in range(0, DIFF_HEADS, ATT_HEAD_GROUP):
            chunk_heads(start, tk, diag_offset, range(first, first + ATT_HEAD_GROUP))

    def chunk_heads(start, tk, diag_offset, group):
        st = {h: lax.dot_general(k_ref[0, pl.ds(start, tk), lanes_of(h)], qs_ref[h], (((1,), (1,)), ((), ())),
                                 preferred_element_type=F32) for h in group}
        if diag_offset is not None:
            key = lax.broadcasted_iota(jnp.int32, (tk, 2 * tq), 0)
            qry = lax.broadcasted_iota(jnp.int32, (tk, 2 * tq), 1) % tq + diag_offset
            st = {h: jnp.where(key <= qry, s, -1e30) for h, s in st.items()}
        m_old = {h: m_ref[h] for h in group}
        m_new = {h: jnp.maximum(m_old[h], jnp.max(st[h], 0, keepdims=True)) for h in group}
        pt = {h: jnp.exp2(st[h] - m_new[h]) for h in group}
        corr = {h: jnp.exp2(m_old[h] - m_new[h]) for h in group}
        pv = {h: lax.dot_general(v_ref[0, pl.ds(start, tk), lanes_of(h)], pt[h].astype(BF16),
                                 (((0,), (0,)), ((), ())), preferred_element_type=F32)
              for h in group}
        for h in group:
            m_ref[h] = m_new[h]
            l_ref[h] = l_ref[h] * corr[h] + jnp.sum(pt[h], 0, keepdims=True)
            acc_ref[h] = acc_ref[h] * corr[h] + pv[h]

    @pl.loop(0, qi // 2)
    def _(j):
        chunk(pl.multiple_of(j * (2 * tq), 2 * tq), 2 * tq, None)

    @pl.when(qi % 2 == 1)
    def _():
        chunk(pl.multiple_of((qi - 1) * tq, tq), 2 * tq, tq)

    @pl.when(qi % 2 == 0)
    def _():
        chunk(pl.multiple_of(qi * tq, tq), tq, 0)

    for h in heads:
        acc = acc_ref[h]
        inv_l = 1.0 / l_ref[h]
        o = acc[:, :tq] * inv_l[:, :tq] - (lam * inv_l[:, tq:]) * acc[:, tq:]
        ms = jnp.mean(o * o, 0, keepdims=True)
        o = o * lax.rsqrt(ms + SUBLN_EPS) * g_ref[...]
        o_ref[0, :, lanes_of(h)] = (o * (1.0 - lam_init)).T.astype(BF16)


def _diff_attn(q, k, v, lq1, lk1, lq2, lk2, subln_g, lam_init):
    B, T, W = q.shape
    tq = TQ_ATT
    vec = lambda a: a.reshape(1, -1)
    vspec = lambda n: pl.BlockSpec((1, n), lambda b, i: (0, 0))
    return pl.pallas_call(
        functools.partial(_diff_attn_kernel, tq=tq, lam_init=lam_init),
        grid=(B, T // tq),
        in_specs=[pl.BlockSpec((1, tq, W), lambda b, i: (b, i, 0)),
                  pl.BlockSpec((1, T, W), lambda b, i: (b, 0, 0)),
                  pl.BlockSpec((1, T, W), lambda b, i: (b, 0, 0)),
                  vspec(DIFF_HEAD_DIM), vspec(DIFF_HEAD_DIM), vspec(DIFF_HEAD_DIM), vspec(DIFF_HEAD_DIM),
                  pl.BlockSpec((2 * DIFF_HEAD_DIM, 1), lambda b, i: (0, 0))],
        out_specs=pl.BlockSpec((1, tq, W), lambda b, i: (b, i, 0)),
        out_shape=jax.ShapeDtypeStruct((B, T, W), BF16),
        scratch_shapes=[pltpu.VMEM((DIFF_HEADS, 2 * tq, LANES), BF16),
                        pltpu.VMEM((DIFF_HEADS, 1, 2 * tq), F32),
                        pltpu.VMEM((DIFF_HEADS, 1, 2 * tq), F32),
                        pltpu.VMEM((DIFF_HEADS, LANES, 2 * tq), F32)],
        compiler_params=_cparams(("parallel", "parallel")),
        name="diff_attention",
    )(q, k, v, vec(lq1), vec(lk1), vec(lq2), vec(lk2), subln_g.reshape(-1, 1))


def _sigmoid(z):
    return 0.5 + 0.5 * jnp.tanh(0.5 * z)


def _rwkv_proj_kernel(*refs, tm, has_vres):
    if has_vres:
        (x_ref, xh_ref, mix_ref, wr_ref, wk_ref, wv_ref, w0_ref, w1_ref, w2_ref, a0_ref, a1_ref, a2_ref,
         g1_ref, g2_ref, v0_ref, v1_ref, v2_ref, vf_ref,
         r_ref, k_ref, v_ref, lw_ref, a_ref, g_ref, xbuf) = refs
    else:
        (x_ref, xh_ref, mix_ref, wr_ref, wk_ref, wv_ref, w0_ref, w1_ref, w2_ref, a0_ref, a1_ref, a2_ref,
         g1_ref, g2_ref,
         r_ref, k_ref, v_ref, lw_ref, a_ref, g_ref, xbuf) = refs
    i = pl.program_id(1)
    x = x_ref[0]
    xbuf[0:SUBLANES, :] = xh_ref[0] * (i > 0).astype(F32)
    xbuf[SUBLANES:, :] = x
    xx = xbuf[pl.ds(SUBLANES - 1, tm), :] - x
    mix = mix_ref[...]

    def mixed(n):
        return (x + xx * mix[n:n + 1]).astype(BF16)

    r_ref[0] = jnp.dot(mixed(0), wr_ref[...], preferred_element_type=F32)
    xw = mixed(1)
    wpre = w0_ref[...] + _bdot(jnp.tanh(jnp.dot(xw, w1_ref[...], preferred_element_type=F32)), w2_ref[...])
    w_log = jnp.minimum(wpre, 0.0) - jnp.log(1.0 + jnp.exp(-jnp.abs(wpre))) - 0.5
    lw_ref[0] = -jnp.exp(w_log)
    k_ref[0] = jnp.dot(mixed(2), wk_ref[...], preferred_element_type=F32)
    xv = mixed(3)
    v = jnp.dot(xv, wv_ref[...], preferred_element_type=F32)
    if has_vres:
        gate = _sigmoid(v0_ref[...] + _bdot(jnp.dot(xv, v1_ref[...], preferred_element_type=F32), v2_ref[...]))
        v = v + (vf_ref[0] - v) * gate
    v_ref[0] = v
    xa = mixed(4)
    a_ref[0] = _sigmoid(a0_ref[...] + _bdot(jnp.dot(xa, a1_ref[...], preferred_element_type=F32), a2_ref[...]))
    xg = mixed(5)
    g_ref[0] = _bdot(_sigmoid(jnp.dot(xg, g1_ref[...], preferred_element_type=F32)), g2_ref[...]).astype(BF16)


def _rwkv_proj(x, mix, w_r, w_k, w_v, w0, w1, w2, a0, a1, a2, g1, g2, vres, v_first):
    B, T, D = x.shape
    tm = TM_RWKV
    halo = _halo_index(tm, SUBLANES)
    has_vres = vres is not None
    full = lambda a: pl.BlockSpec(a.shape, lambda b, i: (0,) * a.ndim, pipeline_mode=pl.Buffered(1))
    tile = pl.BlockSpec((1, tm, D), lambda b, i: (b, i, 0))
    row = lambda a: a.reshape(1, D)
    args = [x, x, mix, w_r, w_k, w_v, row(w0), w1, w2, row(a0), a1, a2, g1, g2]
    if has_vres:
        v0, v1, v2 = vres
        args += [row(v0), v1, v2, v_first]
    in_specs = ([tile, pl.BlockSpec((1, SUBLANES, D), lambda b, i: (b, halo(i), 0))]
                + [full(a) for a in args[2:14]])
    if has_vres:
        in_specs += [full(args[14]), full(args[15]), full(args[16]), tile]
    f32o = jax.ShapeDtypeStruct((B, T, D), F32)
    return pl.pallas_call(
        functools.partial(_rwkv_proj_kernel, tm=tm, has_vres=has_vres),
        grid=(B, T // tm),
        in_specs=in_specs,
        out_specs=[tile] * 6,
        out_shape=[f32o, f32o, f32o, f32o, f32o, jax.ShapeDtypeStruct((B, T, D), BF16)],
        scratch_shapes=[pltpu.VMEM((tm + SUBLANES, D), F32)],
        compiler_params=_cparams(("parallel", "parallel")),
        name="rwkv_proj",
    )(*args)


def _stack_heads(z, head0):
    zero = jnp.zeros_like(z)
    return jnp.concatenate([jnp.where(head0, z, zero), jnp.where(head0, zero, z)], axis=0)


def _head_sum(z, head0):
    zero = jnp.zeros_like(z)
    s0 = jnp.sum(jnp.where(head0, z, zero), -1, keepdims=True)
    s1 = jnp.sum(jnp.where(head0, zero, z), -1, keepdims=True)
    return jnp.where(head0, s0, s1)


def _unit_lower_inverse(Ls, row, col, n_time):
    zero = jnp.zeros_like(Ls[0])
    eye = (row == col).astype(F32)
    base = (row // 2) == (col // 2)
    ts = [eye + jnp.where(base, L, zero) for L in Ls]
    s = 2
    while s < n_time:
        lower_left = ((row // (2 * s)) == (col // (2 * s))) & ((row // s) % 2 == 1) & ((col // s) % 2 == 0)
        offs = [_bdot(t, jnp.where(lower_left, L, zero)) for t, L in zip(ts, Ls)]
        ts = [t + _bdot(o, t) for t, o in zip(ts, offs)]
        s *= 2
    return ts


def _wkv_kernel(r_ref, k_ref, v_ref, lw_ref, a_ref, g_ref, kk_ref, ka_ref, rk_ref, gng_ref, gnb_ref,
                o_ref, st_ref, *, C, n_pairs):
    @pl.when(pl.program_id(1) == 0)
    def _():
        st_ref[...] = jnp.zeros_like(st_ref)

    P = 2 * C
    lane = lax.broadcasted_iota(jnp.int32, (1, LANES), 1)
    head0 = lane < RWKV_HEAD
    row = lax.broadcasted_iota(jnp.int32, (P, P), 0)
    col = lax.broadcasted_iota(jnp.int32, (P, P), 1)
    same_head = (row // C) == (col // C)
    strict = same_head & (col < row)
    incl = same_head & (col <= row)
    trow = lax.broadcasted_iota(jnp.int32, (C, LANES), 0)
    zero_pp = jnp.zeros((P, P), F32)
    pairs = range(o_ref.shape[0] * n_pairs)
    lanes_of = lambda p: slice((p % n_pairs) * LANES, (p % n_pairs + 1) * LANES)

    def prepare(p):
        sl = lanes_of(p)
        r = r_ref[p // n_pairs, :, sl]
        k = k_ref[p // n_pairs, :, sl]
        lw = lw_ref[p // n_pairs, :, sl]
        a = a_ref[p // n_pairs, :, sl]
        kk = k * kk_ref[:, sl]
        norm = jnp.sqrt(_head_sum(kk * kk, head0))
        kk = kk / jnp.maximum(norm, 1e-12)
        k2 = k * (1.0 + (a - 1.0) * ka_ref[:, sl])
        cum = lw
        sh = 1
        while sh < C:
            cum = cum + jnp.where(trow >= sh, pltpu.roll(cum, sh, 0), 0.0)
            sh *= 2
        cum_end = cum[C - 1:C, :]
        e_pos = jnp.exp(cum)
        e_neg = jnp.exp(-cum)
        e_tail = jnp.exp(cum_end - cum)
        b_t = kk * a
        return dict(
            a_s=_stack_heads(-kk * jnp.exp(cum - lw), head0),
            b_s=_stack_heads(b_t * e_neg, head0),
            k_s=_stack_heads(k2 * e_neg, head0),
            r_s=_stack_heads(r * e_pos, head0),
            bh_s=_stack_heads(b_t * e_tail, head0),
            kh_s=_stack_heads(k2 * e_tail, head0),
            v_s=_stack_heads(v_ref[p // n_pairs, :, sl], head0),
            decay_end=jnp.exp(cum_end))

    d, grams = [], []
    for p in pairs:
        d.append(prepare(p))
        grams.append(_bdot_nt(jnp.concatenate([d[p]["a_s"], d[p]["r_s"]], 0),
                              jnp.concatenate([d[p]["b_s"], d[p]["k_s"]], 0)))
    l_ab = [jnp.where(strict, g[:P, :P], zero_pp) for g in grams]
    lv = [_bdot(jnp.where(strict, grams[p][:P, P:], zero_pp), d[p]["v_s"]) for p in pairs]
    m_r = [jnp.concatenate([jnp.where(incl, g[P:, :P], zero_pp), jnp.where(incl, g[P:, P:], zero_pp)], 1)
           for g in grams]
    t_inv = _unit_lower_inverse(l_ab, row, col, C)
    ty = [_bdot(t_inv[p], jnp.concatenate([d[p]["a_s"], lv[p]], 1)) for p in pairs]
    sts = [st_ref[p] for p in pairs]
    zs = [_bdot_nt(jnp.concatenate([ty[p][:, :LANES], d[p]["r_s"]], 0), sts[p]) for p in pairs]
    uvs = [jnp.concatenate([zs[p][:P] + ty[p][:, LANES:], d[p]["v_s"]], 0) for p in pairs]
    o_s = [zs[p][P:] + _bdot(m_r[p], uvs[p]) for p in pairs]
    for p in pairs:
        st_ref[p] = sts[p] * d[p]["decay_end"] + _bdot(uvs[p].T, jnp.concatenate([d[p]["bh_s"], d[p]["kh_s"]], 0))
        sl = lanes_of(p)
        o = o_s[p][:C] + o_s[p][C:]
        r = r_ref[p // n_pairs, :, sl]
        k2 = k_ref[p // n_pairs, :, sl] * (1.0 + (a_ref[p // n_pairs, :, sl] - 1.0) * ka_ref[:, sl])
        mu = _head_sum(o, head0) * (1.0 / RWKV_HEAD)
        oc = o - mu
        var = _head_sum(oc * oc, head0) * (1.0 / RWKV_HEAD)
        on = oc * lax.rsqrt(var + GN_EPS) * gng_ref[:, sl] + gnb_ref[:, sl]
        bonus = _head_sum(r * k2 * rk_ref[:, sl], head0) * v_ref[p // n_pairs, :, sl]
        o_ref[p // n_pairs, :, sl] = ((on + bonus) * g_ref[p // n_pairs, :, sl].astype(F32)).astype(BF16)


def _wkv(r, k, v, lw, a, g, k_k, k_a, r_k, gn_g, gn_b):
    B, T, D = r.shape
    C = WKV_CHUNK
    n_pairs = D // LANES
    nb = WKV_BATCH
    tile = pl.BlockSpec((nb, C, D), lambda b, c: (b, c, 0))
    vspec = pl.BlockSpec((1, D), lambda b, c: (0, 0))
    row = lambda z: z.reshape(1, D)
    return pl.pallas_call(
        functools.partial(_wkv_kernel, C=C, n_pairs=n_pairs),
        grid=(B // nb, T // C),
        in_specs=[tile] * 6 + [vspec] * 5,
        out_specs=tile,
        out_shape=jax.ShapeDtypeStruct((B, T, D), BF16),
        scratch_shapes=[pltpu.VMEM((nb * n_pairs, LANES, LANES), F32)],
        compiler_params=_cparams(("parallel", "arbitrary")),
        name="wkv7",
    )(r, k, v, lw, a, g, row(k_k), row(k_a), row(r_k), row(gn_g), row(gn_b))


def _ffn_kernel(*refs, n_mix, tm, fc, d_ff):
    x_ref, xh_ref = refs[:2]
    a_refs = refs[2:2 + n_mix]
    ah_refs = refs[2 + n_mix:2 + 2 * n_mix]
    wo_refs = refs[2 + 2 * n_mix:2 + 3 * n_mix]
    (g1_ref, b1_ref, wup_ref, cw_ref, cb_ref, wd_ref, g_ref, b_ref,
     o_ref, xcat_ref, hg_ref, hu_ref, act_ref) = refs[2 + 3 * n_mix:15 + 3 * n_mix]
    acat_refs = refs[15 + 3 * n_mix:]
    i = pl.program_id(1)
    mix = None
    for a_ref, ah_ref, wo_ref, acat_ref in zip(a_refs, ah_refs, wo_refs, acat_refs):
        acat_ref[0:HALO, :] = ah_ref[0]
        acat_ref[HALO:, :] = a_ref[0]
        part = jnp.dot(acat_ref[...], wo_ref[...], preferred_element_type=F32)
        mix = part if mix is None else mix + part
    x = _layer_norm_rows(DN_ALPHA * x_ref[0] + mix[HALO:], g1_ref[...], b1_ref[...])
    xh = _layer_norm_rows(DN_ALPHA * xh_ref[0] + mix[:HALO], g1_ref[...], b1_ref[...])
    xcat_ref[0:HALO, :] = (xh * (i > 0).astype(F32)).astype(BF16)
    xcat_ref[HALO:, :] = x.astype(BF16)

    n_chunks = d_ff // fc

    def up_project(c):
        for col, h_ref in ((c * fc, hg_ref), (d_ff + c * fc, hu_ref)):
            h_ref[c % 2] = jnp.dot(xcat_ref[...], wup_ref[:, col:col + fc], preferred_element_type=F32)

    def conv(col, h_ref):
        cw = cw_ref[:, col:col + fc]
        return (cw[2:3] * h_ref[pl.ds(HALO, tm), :] + cw[1:2] * h_ref[pl.ds(HALO - 1, tm), :]
                + cw[0:1] * h_ref[pl.ds(HALO - 2, tm), :] + cb_ref[:, col:col + fc])

    up_project(0)
    for c in range(n_chunks):
        if c + 1 < n_chunks:
            up_project(c + 1)
        gate = conv(c * fc, hg_ref.at[c % 2])
        up = conv(d_ff + c * fc, hu_ref.at[c % 2])
        half = 0.5 * gate
        act_ref[:, c * fc:(c + 1) * fc] = (half * (1.0 + jnp.tanh(half)) * up).astype(BF16)
    y = jnp.dot(act_ref[...], wd_ref[...], preferred_element_type=F32)
    o_ref[0] = _layer_norm_rows(DN_ALPHA * x + y, g_ref[...], b_ref[...])


def _ffn(x, a_list, w_list, ln1_g, ln1_b, w_up, conv_w, conv_b, w_down, ln_g, ln_b):
    B, T, D = x.shape
    d_ff = w_down.shape[0]
    tm, fc = TM_FFN, FC_FFN
    assert d_ff % fc == 0
    halo = _halo_index(tm, HALO)
    const = lambda a: pl.BlockSpec(a.shape, lambda b, i: (0,) * a.ndim, pipeline_mode=pl.Buffered(1))
    tile = lambda k: pl.BlockSpec((1, tm, k), lambda b, i: (b, i, 0))
    hist = lambda k: pl.BlockSpec((1, HALO, k), lambda b, i: (b, halo(i), 0))
    row = lambda z: z.reshape(1, -1)
    consts = [row(ln1_g), row(ln1_b), w_up, conv_w, row(conv_b), w_down, row(ln_g), row(ln_b)]
    return pl.pallas_call(
        functools.partial(_ffn_kernel, n_mix=len(a_list), tm=tm, fc=fc, d_ff=d_ff),
        grid=(B, T // tm),
        in_specs=([tile(D), hist(D)] + [tile(a.shape[2]) for a in a_list] + [hist(a.shape[2]) for a in a_list]
                  + [const(w) for w in w_list] + [const(c) for c in consts]),
        out_specs=tile(D),
        out_shape=jax.ShapeDtypeStruct((B, T, D), F32),
        scratch_shapes=([pltpu.VMEM((tm + HALO, D), BF16),
                         pltpu.VMEM((2, tm + HALO, fc), F32), pltpu.VMEM((2, tm + HALO, fc), F32),
                         pltpu.VMEM((tm, d_ff), BF16)]
                        + [pltpu.VMEM((tm + HALO, a.shape[2]), BF16) for a in a_list]),
        compiler_params=_cparams(("parallel", "parallel")),
        name="mix_out_conv_glu_ffn",
    )(x, x, *a_list, *a_list, *w_list, *consts)


def _rope_tables(T):
    half = DIFF_HEAD_DIM // 2
    inv = 1.0 / (ROPE_THETA ** (jnp.arange(0, DIFF_HEAD_DIM, 2, dtype=F32) / DIFF_HEAD_DIM))
    ang = jnp.arange(T, dtype=F32)[:, None] * inv[None, :]
    cos, sin = jnp.cos(ang), jnp.sin(ang)
    reps = LANES // DIFF_HEAD_DIM
    cos_t = jnp.tile(jnp.concatenate([cos, cos], -1), (1, reps))
    sin_t = jnp.tile(jnp.concatenate([-sin, sin], -1), (1, reps))
    assert cos_t.shape == (T, LANES) and half * 2 == DIFF_HEAD_DIM
    return cos_t, sin_t


def _even_layer(x, w_in, conv_w, lq1, lk1, lq2, lk2, subln_g, w_out, lam_init, cos_t, sin_t, ln_g, ln_b):
    B, T, D = x.shape
    W = CONV_WIDTH_MIX
    y_a, q, k, v = _even_in(x, w_in.astype(BF16), conv_w, cos_t, sin_t)
    y_b = _diff_attn(q, k, v, lq1, lk1, lq2, lk2, subln_g, lam_init)
    w_out = w_out.astype(BF16)
    return [y_a, y_b], [w_out[:W], w_out[W:]]


def _odd_layer(x, mix, w_r, w_k, w_v, w_o, w0, w1, w2, a0, a1, a2, g1, g2, k_k, k_a, r_k, gn_g, gn_b,
               v_first, vres, ln_g, ln_b):
    B, T, D = x.shape
    bf = lambda z: z.astype(BF16)
    if vres is not None:
        vres = (vres[0], bf(vres[1]), bf(vres[2]))
    r, k, v, lw, a, g = _rwkv_proj(x, mix, bf(w_r), bf(w_k), bf(w_v), w0, bf(w1), bf(w2), a0, bf(a1), bf(a2),
                                   bf(g1), bf(g2), vres, v_first)
    if vres is None:
        v_first = v
    o = _wkv(r, k, v, lw, a, g, k_k, k_a, r_k.reshape(-1), gn_g, gn_b)
    return [o], [bf(w_o)], v_first


def kernel(x, ev_w_in, ev_conv_w, ev_lam_q1, ev_lam_k1, ev_lam_q2, ev_lam_k2, ev_subln_g, ev_w_out, rw_mix, rw_w_r, rw_w_k, rw_w_v, rw_w_o, rw_w0, rw_w1, rw_w2, rw_a0, rw_a1, rw_a2, rw_g1, rw_g2, rw_k_k, rw_k_a, rw_r_k, rw_gn_g, rw_gn_b, rw_v0, rw_v1, rw_v2, ffn_w_up, ffn_conv_w, ffn_conv_b, ffn_w_down, ln1_g, ln1_b, ln2_g, ln2_b):
    T = x.shape[1]
    cos_t, sin_t = _rope_tables(T)
    v_first = None
    for l in range(DEPTH):
        if l % 2 == 0:
            i = l // 2
            lam_init = 0.8 - 0.6 * math.exp(-0.3 * l)
            a_list, w_list = _even_layer(x, ev_w_in[i], ev_conv_w[i], ev_lam_q1[i], ev_lam_k1[i], ev_lam_q2[i],
                                         ev_lam_k2[i], ev_subln_g[i], ev_w_out[i], lam_init, cos_t, sin_t,
                                         ln1_g[l], ln1_b[l])
        else:
            j = l // 2
            vres = None if j == 0 else (rw_v0[j - 1], rw_v1[j - 1], rw_v2[j - 1])
            a_list, w_list, v_first = _odd_layer(
                x, rw_mix[j], rw_w_r[j], rw_w_k[j], rw_w_v[j], rw_w_o[j], rw_w0[j],
                rw_w1[j], rw_w2[j], rw_a0[j], rw_a1[j], rw_a2[j], rw_g1[j], rw_g2[j],
                rw_k_k[j], rw_k_a[j], rw_r_k[j], rw_gn_g[j], rw_gn_b[j], v_first, vres, ln1_g[l], ln1_b[l])
        x = _ffn(x, a_list, w_list, ln1_g[l], ln1_b[l], ffn_w_up[l].astype(BF16), ffn_conv_w[l],
                 ffn_conv_b[l], ffn_w_down[l].astype(BF16), ln2_g[l], ln2_b[l])
    return x
```
